```python
import math
import jax, jax.numpy as jnp
from jax import lax
import numpy as np

D_MODEL = 1024
BATCH = 8
SEQ = 2048
DEPTH = 4

HEAD_DIM = 64
BLOCK = 128
RMS_EPS = 1e-6
SWA_Q_HEADS = 8
SWA_KV_HEADS = 2
SWA_WINDOW = 128
FOX_HEADS = 8
SSM_EXPAND = 2
SSM_D_INNER = SSM_EXPAND * D_MODEL
SSM_HEAD_DIM = 64
SSM_HEADS = SSM_D_INNER // SSM_HEAD_DIM
SSM_GROUPS = 4
SSM_STATE = 128
SSM_CONV = 4
SSM_CHUNK = 128
D_FF = 2816

N_ATTN_LAYERS = (DEPTH + 1) // 2
N_SSM_LAYERS = DEPTH // 2

SWA_Q_W = SWA_Q_HEADS * HEAD_DIM
SWA_KV_W = SWA_KV_HEADS * HEAD_DIM
FOX_W = FOX_HEADS * HEAD_DIM
ATTN_SPLITS = np.cumsum([SWA_Q_W, SWA_KV_W, SWA_KV_W, FOX_W, FOX_W, FOX_W]).tolist()
ATTN_IN_W = SWA_Q_W + 2 * SWA_KV_W + 3 * FOX_W + FOX_HEADS
ATTN_OUT_W = SWA_Q_W + FOX_W
SSM_BC_W = SSM_GROUPS * SSM_STATE
SSM_CONV_W = SSM_D_INNER + 2 * SSM_BC_W
SSM_IN_W = SSM_D_INNER + SSM_CONV_W + SSM_HEADS

kernel_name = "hybrid_swa_fox_mamba2_macaron"


def rmsnorm(x, g):
    xf = x.astype(jnp.float32)
    y = xf * lax.rsqrt(jnp.mean(xf * xf, axis=-1, keepdims=True) + RMS_EPS)
    return (y * g.astype(jnp.float32)).astype(x.dtype)


def swiglu(x, w_gate, w_up, w_down):
    return (jax.nn.silu(x @ w_gate) * (x @ w_up)) @ w_down


def sliding_window_sink_attention(q, k, v, sinks):
    b, s, hq, d = q.shape
    nb = s // BLOCK
    g = hq // SWA_KV_HEADS
    f32 = jnp.float32
    qb = q.astype(f32).reshape(b, nb, BLOCK, SWA_KV_HEADS, g, d)

    def with_prev(t):
        tb = t.astype(f32).reshape(b, nb, BLOCK, SWA_KV_HEADS, d)
        prev = jnp.pad(tb, ((0, 0), (1, 0), (0, 0), (0, 0), (0, 0)))[:, :-1]
        return jnp.concatenate([prev, tb], axis=2)

    kc, vc = with_prev(k), with_prev(v)
    scores = jnp.einsum('bnqkgd,bnskd->bnkgqs', qb, kc) * (d ** -0.5)
    qpos = jnp.arange(BLOCK)[:, None] + BLOCK
    kpos = jnp.arange(2 * BLOCK)[None, :]
    rel = qpos - kpos
    in_window = (rel >= 0) & (rel < SWA_WINDOW)
    blk = jnp.arange(nb)[:, None, None]
    valid_key = (blk * BLOCK + kpos[None] - BLOCK) >= 0
    mask = in_window[None] & valid_key
    scores = jnp.where(mask[None, :, None, None], scores, -jnp.inf)
    sink = sinks.astype(f32).reshape(1, 1, SWA_KV_HEADS, g, 1, 1)
    m = jnp.maximum(jnp.max(scores, axis=-1, keepdims=True), sink)
    p = jnp.exp(scores - m)
    denom = jnp.sum(p, axis=-1, keepdims=True) + jnp.exp(sink - m)
    o = jnp.einsum('bnkgqs,bnskd->bnkgqd', p, vc) / denom
    o = o.transpose(0, 1, 4, 2, 3, 5).reshape(b, s, hq * d)
    return o.astype(q.dtype)


def forgetting_attention(q, k, v, log_f):
    b, s, h, d = q.shape
    nb = s // BLOCK
    f32 = jnp.float32
    cum_f = jnp.cumsum(log_f, axis=1)
    kh = k.astype(f32).transpose(0, 2, 1, 3)
    vh = v.astype(f32).transpose(0, 2, 1, 3)
    f_k = cum_f.transpose(0, 2, 1)
    qb = q.astype(f32).reshape(b, nb, BLOCK, h, d).transpose(1, 0, 3, 2, 4)
    f_q = cum_f.reshape(b, nb, BLOCK, h).transpose(1, 0, 3, 2)
    kpos = jnp.arange(s)
    scale = d ** -0.5

    def one_block(args):
        qi, fqi, i = args
        sc = jnp.einsum('bhqd,bhsd->bhqs', qi, kh) * scale + fqi[..., None] - f_k[:, :, None, :]
        qpos = i * BLOCK + jnp.arange(BLOCK)
        sc = jnp.where(kpos[None, :] <= qpos[:, None], sc, -jnp.inf)
        p = jax.nn.softmax(sc, axis=-1)
        return jnp.einsum('bhqs,bhsd->bhqd', p, vh)

    o = lax.map(one_block, (qb, f_q, jnp.arange(nb)))
    return o.transpose(1, 0, 3, 2, 4).reshape(b, s, h * d).astype(q.dtype)


def attention_mixer(u, w_in, forget_bias, swa_q_norm, swa_k_norm, swa_sinks,
                    fox_q_norm, fox_k_norm, w_out):
    b, s, _ = u.shape
    proj = u @ w_in
    qa, ka, va, qf, kf, vf, fl = jnp.split(proj, ATTN_SPLITS, axis=-1)
    qa = rmsnorm(qa.reshape(b, s, SWA_Q_HEADS, HEAD_DIM), swa_q_norm)
    ka = rmsnorm(ka.reshape(b, s, SWA_KV_HEADS, HEAD_DIM), swa_k_norm)
    va = va.reshape(b, s, SWA_KV_HEADS, HEAD_DIM)
    o_a = sliding_window_sink_attention(qa, ka, va, swa_sinks)
    qf = rmsnorm(qf.reshape(b, s, FOX_HEADS, HEAD_DIM), fox_q_norm)
    kf = rmsnorm(kf.reshape(b, s, FOX_HEADS, HEAD_DIM), fox_k_norm)
    vf = vf.reshape(b, s, FOX_HEADS, HEAD_DIM)
    log_f = jax.nn.log_sigmoid(fl.astype(jnp.float32) + forget_bias.astype(jnp.float32))
    o_b = forgetting_attention(qf, kf, vf, log_f)
    return jnp.concatenate([o_a, o_b], axis=-1) @ w_out


def causal_depthwise_conv(x, w, bias):
    k, c = w.shape
    out = lax.conv_general_dilated(
        x, w[:, None, :].astype(x.dtype), window_strides=(1,), padding=((k - 1, 0),),
        dimension_numbers=('NWC', 'WIO', 'NWC'), feature_group_count=c)
    return out + bias


def ssd_chunked(xs, dt, a, b_in, c_in):
    bsz, s, h, p = xs.shape
    g, n = b_in.shape[2], b_in.shape[3]
    r = h // g
    nc, l = s // SSM_CHUNK, SSM_CHUNK
    f32 = jnp.float32
    xd = (xs.astype(f32) * dt[..., None]).reshape(bsz, nc, l, g, r, p)
    ad = (dt * a).reshape(bsz, nc, l, g, r).transpose(0, 3, 4, 1, 2)
    bc = b_in.astype(f32).reshape(bsz, nc, l, g, n)
    cc = c_in.astype(f32).reshape(bsz, nc, l, g, n)
    a_cs = jnp.cumsum(ad, axis=-1)
    causal = jnp.tril(jnp.ones((l, l), dtype=bool))
    seg = a_cs[..., :, None] - a_cs[..., None, :]
    decay_in = jnp.exp(jnp.where(causal, seg, -jnp.inf))
    cb = jnp.einsum('bclgn,bcsgn->bgcls', cc, bc)
    y_diag = jnp.einsum('bgrcls,bcsgrp->bclgrp', cb[:, :, None] * decay_in, xd)
    decay_to_end = jnp.exp(a_cs[..., -1:] - a_cs).transpose(0, 3, 4, 1, 2)
    states = jnp.einsum('bclgn,bclgrp->cbgrpn', bc, xd * decay_to_end[..., None])
    chunk_decay = jnp.exp(a_cs[..., -1]).transpose(3, 0, 1, 2)

    def step(hstate, inp):
        st, dec = inp
        return hstate * dec[..., None, None] + st, hstate

    h0 = jnp.zeros((bsz, g, r, p, n), f32)
    _, h_prev = lax.scan(step, h0, (states, chunk_decay))
    decay_from_start = jnp.exp(a_cs).transpose(0, 3, 4, 1, 2)
    y_off = jnp.einsum('bclgn,cbgrpn->bclgrp', cc, h_prev) * decay_from_start[..., None]
    return (y_diag + y_off).reshape(bsz, s, h, p)


def mamba2_mixer(u, w_in, conv_w, conv_b, dt_bias, a_log, d_skip, norm_g, w_out):
    b, s, _ = u.shape
    f32 = jnp.float32
    proj = u @ w_in
    z, xbc, dt = jnp.split(proj, [SSM_D_INNER, SSM_D_INNER + SSM_CONV_W], axis=-1)
    xbc = jax.nn.silu(causal_depthwise_conv(xbc, conv_w, conv_b))
    xs, b_in, c_in = jnp.split(xbc, [SSM_D_INNER, SSM_D_INNER + SSM_BC_W], axis=-1)
    dt = jax.nn.softplus(dt.astype(f32) + dt_bias.astype(f32))
    a = -jnp.exp(a_log.astype(f32))
    xs = xs.reshape(b, s, SSM_HEADS, SSM_HEAD_DIM)
    y = ssd_chunked(xs, dt, a,
                    b_in.reshape(b, s, SSM_GROUPS, SSM_STATE),
                    c_in.reshape(b, s, SSM_GROUPS, SSM_STATE))
    y = y + d_skip.astype(f32)[:, None] * xs.astype(f32)
    y = y.reshape(b, s, SSM_D_INNER) * jax.nn.silu(z.astype(f32))
    yg = y.reshape(b, s, SSM_GROUPS, SSM_D_INNER // SSM_GROUPS)
    yg = yg * lax.rsqrt(jnp.mean(yg * yg, axis=-1, keepdims=True) + RMS_EPS)
    y = yg.reshape(b, s, SSM_D_INNER) * norm_g.astype(f32)
    return y.astype(u.dtype) @ w_out


def setup_inputs(seed: int = 0) -> dict:
    key = jax.random.key(seed)
    ks = iter(jax.random.split(key, 40))
    f32 = jnp.float32

    def nrm(shape, scale):
        return jax.random.normal(next(ks), shape, f32) * scale

    def gain(shape):
        return 1.0 + nrm(shape, 0.02)

    def uni(shape, lo, hi):
        return jax.random.uniform(next(ks), shape, f32, minval=lo, maxval=hi)

    L, NA, NS = DEPTH, N_ATTN_LAYERS, N_SSM_LAYERS
    dt0 = jnp.exp(uni((NS, SSM_HEADS), math.log(1e-3), math.log(1e-1)))
    return {
        "x": nrm((BATCH, SEQ, D_MODEL), 1.0),
        "ffn1_norm": gain((L, D_MODEL)),
        "ffn1_w_gate": nrm((L, D_MODEL, D_FF), D_MODEL ** -0.5),
        "ffn1_w_up": nrm((L, D_MODEL, D_FF), D_MODEL ** -0.5),
        "ffn1_w_down": nrm((L, D_FF, D_MODEL), D_FF ** -0.5),
        "mix_norm": gain((L, D_MODEL)),
        "ffn2_norm": gain((L, D_MODEL)),
        "ffn2_w_gate": nrm((L, D_MODEL, D_FF), D_MODEL ** -0.5),
        "ffn2_w_up": nrm((L, D_MODEL, D_FF), D_MODEL ** -0.5),
        "ffn2_w_down": nrm((L, D_FF, D_MODEL), D_FF ** -0.5),
        "attn_w_in": nrm((NA, D_MODEL, ATTN_IN_W), D_MODEL ** -0.5),
        "attn_forget_bias": uni((NA, FOX_HEADS), 2.0, 5.0),
        "swa_q_norm": gain((NA, HEAD_DIM)),
        "swa_k_norm": gain((NA, HEAD_DIM)),
        "swa_sinks": nrm((NA, SWA_Q_HEADS), 0.5),
        "fox_q_norm": gain((NA, HEAD_DIM)),
        "fox_k_norm": gain((NA, HEAD_DIM)),
        "attn_w_out": nrm((NA, ATTN_OUT_W, D_MODEL), ATTN_OUT_W ** -0.5),
        "ssm_w_in": nrm((NS, D_MODEL, SSM_IN_W), D_MODEL ** -0.5),
        "ssm_conv_w": nrm((NS, SSM_CONV, SSM_CONV_W), SSM_CONV ** -0.5),
        "ssm_conv_b": nrm((NS, SSM_CONV_W), 0.02),
        "ssm_dt_bias": dt0 + jnp.log(-jnp.expm1(-dt0)),
        "ssm_a_log": jnp.log(uni((NS, SSM_HEADS), 1.0, 16.0)),
        "ssm_d_skip": gain((NS, SSM_HEADS)),
        "ssm_norm": gain((NS, SSM_D_INNER)),
        "ssm_w_out": nrm((NS, SSM_D_INNER, D_MODEL), SSM_D_INNER ** -0.5),
    }


def reference(x, ffn1_norm, ffn1_w_gate, ffn1_w_up, ffn1_w_down, mix_norm,
              ffn2_norm, ffn2_w_gate, ffn2_w_up, ffn2_w_down,
              attn_w_in, attn_forget_bias, swa_q_norm, swa_k_norm, swa_sinks,
              fox_q_norm, fox_k_norm, attn_w_out,
              ssm_w_in, ssm_conv_w, ssm_conv_b, ssm_dt_bias, ssm_a_log,
              ssm_d_skip, ssm_norm, ssm_w_out):
    h = x
    for layer in range(DEPTH):
        h = h + 0.5 * swiglu(rmsnorm(h, ffn1_norm[layer]),
                             ffn1_w_gate[layer], ffn1_w_up[layer], ffn1_w_down[layer])
        u = rmsnorm(h, mix_norm[layer])
        i = layer // 2
        if layer % 2 == 0:
            h = h + attention_mixer(u, attn_w_in[i], attn_forget_bias[i],
                                    swa_q_norm[i], swa_k_norm[i], swa_sinks[i],
                                    fox_q_norm[i], fox_k_norm[i], attn_w_out[i])
        else:
            h = h + mamba2_mixer(u, ssm_w_in[i], ssm_conv_w[i], ssm_conv_b[i],
                                 ssm_dt_bias[i], ssm_a_log[i], ssm_d_skip[i],
                                 ssm_norm[i], ssm_w_out[i])
        h = h + 0.5 * swiglu(rmsnorm(h, ffn2_norm[layer]),
                             ffn2_w_gate[layer], ffn2_w_up[layer], ffn2_w_down[layer])
    return h
```

```python
import functools
import math

import jax
import jax.numpy as jnp
from jax import lax
from jax.experimental import pallas as pl
from jax.experimental.pallas import tpu as pltpu

F32 = jnp.float32
BF16 = jnp.bfloat16

RMS_EPS = 1e-6
HEAD_DIM = 64
LANES = 128
SWA_Q_HEADS = 8
SWA_KV_HEADS = 2
SWA_BLOCK = 128
FOX_HEADS = 8
SSM_HEADS = 32
SSM_GROUPS = 4
SSM_STATE = 128
SSM_CHUNK = 128
SSM_CONV = 4
NEG_BIG = -1e30

VMEM_LIMIT = 56 * 1024 * 1024


def _cparams(n_axes, sequential=False):
    sem = ("arbitrary",) * n_axes if sequential else ("parallel",) * n_axes
    return pltpu.CompilerParams(dimension_semantics=sem, vmem_limit_bytes=VMEM_LIMIT)


def _resident(shape):
    nd = len(shape)
    return pl.BlockSpec(shape, lambda *_: (0,) * nd, pipeline_mode=pl.Buffered(1))


def _rmsnorm(x, g):
    ms = jnp.mean(x * x, axis=-1, keepdims=True)
    return x * lax.rsqrt(ms + RMS_EPS) * g


def _sigmoid(x):
    return 1.0 / (1.0 + jnp.exp(-x))


def _softplus(x):
    return jnp.maximum(x, 0.0) + jnp.log1p(jnp.exp(-jnp.abs(x)))


def _lane_lo(shape):
    return lax.broadcasted_iota(jnp.int32, shape, len(shape) - 1) < HEAD_DIM


def _head_rmsnorm(x, g):
    lo = _lane_lo(x.shape)
    sq = x * x
    s_lo = jnp.sum(jnp.where(lo, sq, 0.0), axis=-1, keepdims=True)
    s_hi = jnp.sum(jnp.where(lo, 0.0, sq), axis=-1, keepdims=True)
    ms = jnp.where(lo, s_lo, s_hi) * (1.0 / HEAD_DIM)
    return x * lax.rsqrt(ms + RMS_EPS) * g


def _dot(a, b):
    return jnp.dot(a, b, preferred_element_type=F32)


def _dot_nt(a, b):
    return lax.dot_general(a, b, (((1,), (1,)), ((), ())), preferred_element_type=F32)


def _dot_f32(a, b):
    return jnp.dot(a, b, preferred_element_type=F32, precision=lax.Precision.HIGHEST)


def _lower_tri(n):
    r = lax.broadcasted_iota(jnp.int32, (n, n), 0)
    c = lax.broadcasted_iota(jnp.int32, (n, n), 1)
    return r >= c


def _ffn_body(h_ref, g_ref, wg_ref, wu_ref, wd_ref, o_ref):
    x = h_ref[...]
    xn = _rmsnorm(x, g_ref[...]).astype(BF16)
    gate = _dot(xn, wg_ref[...])
    up = _dot(xn, wu_ref[...])
    act = (gate * _sigmoid(gate) * up).astype(BF16)
    o_ref[...] = x + 0.5 * _dot(act, wd_ref[...])


def _ffn(h, g, wg, wu, wd, *, tm):
    t, d = h.shape
    dff = wg.shape[1]
    return pl.pallas_call(
        _ffn_body,
        out_shape=jax.ShapeDtypeStruct((t, d), F32),
        grid=(t // tm,),
        in_specs=[
            pl.BlockSpec((tm, d), lambda i: (i, 0)),
            _resident((1, d)),
            _resident((d, dff)),
            _resident((d, dff)),
            _resident((dff, d)),
        ],
        out_specs=pl.BlockSpec((tm, d), lambda i: (i, 0)),
        compiler_params=_cparams(1),
        name="ffn",
    )(h, g, wg, wu, wd)


ATT_NORM_Q_SWA = (0, 1, 2, 3)
ATT_NORM_K_SWA = (4, 5)
ATT_COPY = (6, 7, 16, 17, 18, 19)
ATT_NORM_Q_FOX = (8, 9, 10, 11)
ATT_NORM_K_FOX = (12, 13, 14, 15)
ATT_TILES = 20


def _attn_in_body(h_ref, g_ref, w_ref, wf_ref, fb_ref, gains_ref, qkv_ref, cum_ref, carry_ref):
    i = pl.program_id(1)

    @pl.when(i == 0)
    def _():
        carry_ref[...] = jnp.zeros_like(carry_ref)

    x = h_ref[...]
    u = _rmsnorm(x, g_ref[...]).astype(BF16)
    proj = _dot(u, w_ref[...])
    scale = HEAD_DIM ** -0.5

    def tile(k):
        return proj[:, k * LANES:(k + 1) * LANES]

    def put(k, v):
        qkv_ref[:, k * LANES:(k + 1) * LANES] = v.astype(BF16)

    for k in ATT_NORM_Q_SWA:
        put(k, _head_rmsnorm(tile(k), gains_ref[0:1, :]) * scale)
    for k in ATT_NORM_K_SWA:
        put(k, _head_rmsnorm(tile(k), gains_ref[1:2, :]))
    for k in ATT_NORM_Q_FOX:
        put(k, _head_rmsnorm(tile(k), gains_ref[2:3, :]) * scale)
    for k in ATT_NORM_K_FOX:
        put(k, _head_rmsnorm(tile(k), gains_ref[3:4, :]))
    for k in ATT_COPY:
        put(k, tile(k))

    fl = _dot(u, wf_ref[...]) + fb_ref[...]
    log_f = -_softplus(-fl)
    tm = log_f.shape[0]
    tri = _lower_tri(tm).astype(F32)
    cum = _dot_f32(tri, log_f) + carry_ref[0:1, :]
    cum_ref[...] = cum
    carry_ref[0:1, :] = cum[tm - 1:tm, :]


def _attn_in(h, g, w, wf, fb, gains, *, tm):
    b, s, d = h.shape
    n = w.shape[1]
    return pl.pallas_call(
        _attn_in_body,
        out_shape=(jax.ShapeDtypeStruct((b, s, n), BF16),
                   jax.ShapeDtypeStruct((b, s, LANES), F32)),
        grid=(b, s // tm),
        in_specs=[
            pl.BlockSpec((None, tm, d), lambda bi, i: (bi, i, 0)),
            _resident((1, d)),
            _resident((d, n)),
            _resident((d, LANES)),
            _resident((1, LANES)),
            _resident((4, LANES)),
        ],
        out_specs=(pl.BlockSpec((None, tm, n), lambda bi, i: (bi, i, 0)),
                   pl.BlockSpec((None, tm, LANES), lambda bi, i: (bi, i, 0))),
        scratch_shapes=[pltpu.VMEM((8, LANES), F32)],
        compiler_params=_cparams(2, sequential=True),
        name="attn_in",
    )(h, g, w, wf, fb, gains)


def _swa_body(sinks_ref, q_ref, kvc_ref, kvp_ref, o_ref):
    n = pl.program_id(1)
    blk = SWA_BLOCK
    q = q_ref[...]
    kv = jnp.concatenate([kvp_ref[...], kvc_ref[...]], axis=0)
    qi = lax.broadcasted_iota(jnp.int32, (blk, 2 * blk), 0) + blk
    ki = lax.broadcasted_iota(jnp.int32, (blk, 2 * blk), 1)
    rel = qi - ki
    mask = (rel >= 0) & (rel < blk) & ((ki >= blk) | (n > 0))
    lo = _lane_lo((blk, LANES))
    group = SWA_Q_HEADS // SWA_KV_HEADS
    for t in range(SWA_Q_HEADS // 2):
        qt = q[:, t * LANES:(t + 1) * LANES]
        outs = []
        for j in range(2):
            head = 2 * t + j
            kv_head = head // group
            var = 0 if kv_head == j else 1
            kk = kv[:, var * LANES:(var + 1) * LANES]
            vv = kv[:, (2 + var) * LANES:(3 + var) * LANES]
            qm = jnp.where(lo if j == 0 else ~lo, qt, jnp.zeros_like(qt))
            sc = _dot_nt(qm, kk)
            sc = jnp.where(mask, sc, -jnp.inf)
            sink = sinks_ref[head]
            m = jnp.maximum(jnp.max(sc, axis=-1, keepdims=True), sink)
            p = jnp.exp(sc - m)
            denom = jnp.sum(p, axis=-1, keepdims=True) + jnp.exp(sink - m)
            outs.append(_dot(p.astype(BF16), vv) / denom)
        o_ref[:, t * LANES:(t + 1) * LANES] = jnp.where(lo, outs[0], outs[1]).astype(BF16)


def _swa(sinks, qkv):
    b, s, _ = qkv.shape
    nb = s // SWA_BLOCK
    qw = SWA_Q_HEADS * HEAD_DIM
    return pl.pallas_call(
        _swa_body,
        out_shape=jax.ShapeDtypeStruct((b, s, qw), BF16),
        grid_spec=pltpu.PrefetchScalarGridSpec(
            num_scalar_prefetch=1,
            grid=(b, nb),
            in_specs=[
                pl.BlockSpec((None, SWA_BLOCK, qw), lambda bi, i, sk: (bi, i, 0)),
                pl.BlockSpec((None, SWA_BLOCK, qw), lambda bi, i, sk: (bi, i, 1)),
                pl.BlockSpec((None, SWA_BLOCK, qw), lambda bi, i, sk: (bi, jnp.maximum(i - 1, 0), 1)),
            ],
            out_specs=pl.BlockSpec((None, SWA_BLOCK, qw), lambda bi, i, sk: (bi, i, 0)),
        ),
        compiler_params=_cparams(2),
        name="swa",
    )(sinks, qkv, qkv, qkv)


FOX_Q_TILE0 = 8
FOX_K_TILE0 = 12
FOX_V_TILE0 = 16


def _fox_body(q_ref, k_ref, v_ref, fq_ref, fk_ref, o_ref, *, tq):
    hp = pl.program_id(1)
    i = pl.program_id(2)
    q = q_ref[...]
    fq_all = fq_ref[...]
    lo = _lane_lo((tq, LANES))
    lane = lax.broadcasted_iota(jnp.int32, (tq, LANES), 1)
    causal = _lower_tri(tq)
    outs = []
    for j in range(2):
        head = 2 * hp + j
        qm = jnp.where(lo if j == 0 else ~lo, q, jnp.zeros_like(q))
        fq = jnp.sum(jnp.where(lane == head, fq_all, 0.0), axis=-1, keepdims=True)

        def scores(jb):
            start = pl.multiple_of(jb * tq, tq)
            kb = k_ref[pl.ds(start, tq), :]
            vb = v_ref[pl.ds(start, tq), :]
            fk = fk_ref[jb, j:j + 1, :]
            return _dot_nt(qm, kb) + (fq - fk), vb

        def update(carry, sc, vb):
            m, l, acc = carry
            m_new = jnp.maximum(m, jnp.max(sc, axis=-1, keepdims=True))
            alpha = jnp.exp(m - m_new)
            p = jnp.exp(sc - m_new)
            l_new = alpha * l + jnp.sum(p, axis=-1, keepdims=True)
            acc_new = alpha * acc + _dot(p.astype(BF16), vb)
            return m_new, l_new, acc_new

        def step(jb, carry):
            sc, vb = scores(jb)
            return update(carry, sc, vb)

        init = (jnp.full((tq, 1), NEG_BIG, F32), jnp.zeros((tq, 1), F32),
                jnp.zeros((tq, LANES), F32))
        carry = lax.fori_loop(0, i, step, init)
        sc, vb = scores(i)
        sc = jnp.where(causal, sc, NEG_BIG)
        m, l, acc = update(carry, sc, vb)
        outs.append(acc / l)
    o_ref[...] = jnp.where(lo, outs[0], outs[1]).astype(BF16)


def _fox(qkv, cum_col, cum_row, *, tq):
    b, s, _ = qkv.shape
    nq = s // tq
    npairs = FOX_HEADS // 2
    return pl.pallas_call(
        functools.partial(_fox_body, tq=tq),
        out_shape=jax.ShapeDtypeStruct((b, s, FOX_HEADS * HEAD_DIM), BF16),
        grid=(b, npairs, nq),
        in_specs=[
            pl.BlockSpec((None, tq, LANES), lambda bi, hp, i: (bi, i, FOX_Q_TILE0 + hp)),
            pl.BlockSpec((None, s, LANES), lambda bi, hp, i: (bi, 0, FOX_K_TILE0 + hp)),
            pl.BlockSpec((None, s, LANES), lambda bi, hp, i: (bi, 0, FOX_V_TILE0 + hp)),
            pl.BlockSpec((None, tq, LANES), lambda bi, hp, i: (bi, i, 0)),
            pl.BlockSpec((None, None, nq, 2, tq), lambda bi, hp, i: (bi, hp, 0, 0, 0)),
        ],
        out_specs=pl.BlockSpec((None, tq, LANES), lambda bi, hp, i: (bi, i, hp)),
        compiler_params=_cparams(3),
        name="fox",
    )(qkv, qkv, qkv, cum_col, cum_row)


def _out2_body(h_ref, a_ref, b_ref, wa_ref, wb_ref, o_ref):
    o_ref[...] = h_ref[...] + _dot(a_ref[...], wa_ref[...]) + _dot(b_ref[...], wb_ref[...])


def _attn_out(h, oa, ob, wa, wb, *, tm):
    t, d = h.shape
    ka, kb = oa.shape[1], ob.shape[1]
    return pl.pallas_call(
        _out2_body,
        out_shape=jax.ShapeDtypeStruct((t, d), F32),
        grid=(t // tm,),
        in_specs=[
            pl.BlockSpec((tm, d), lambda i: (i, 0)),
            pl.BlockSpec((tm, ka), lambda i: (i, 0)),
            pl.BlockSpec((tm, kb), lambda i: (i, 0)),
            _resident((ka, d)),
            _resident((kb, d)),
        ],
        out_specs=pl.BlockSpec((tm, d), lambda i: (i, 0)),
        compiler_params=_cparams(1),
        name="attn_out",
    )(h, oa, ob, wa, wb)


def _out1_body(h_ref, a_ref, w_ref, o_ref):
    o_ref[...] = h_ref[...] + _dot(a_ref[...], w_ref[...])


def _ssm_out(h, y, w, *, tm):
    t, d = h.shape
    k = y.shape[1]
    return pl.pallas_call(
        _out1_body,
        out_shape=jax.ShapeDtypeStruct((t, d), F32),
        grid=(t // tm,),
        in_specs=[
            pl.BlockSpec((tm, d), lambda i: (i, 0)),
            pl.BlockSpec((tm, k), lambda i: (i, 0)),
            _resident((k, d)),
        ],
        out_specs=pl.BlockSpec((tm, d), lambda i: (i, 0)),
        compiler_params=_cparams(1),
        name="ssm_out",
    )(h, y, w)


def _ssm_in_body(h_ref, g_ref, wz_ref, wx_ref, wdt_ref, z_ref, x_ref, dt_ref):
    u = _rmsnorm(h_ref[...], g_ref[...]).astype(BF16)
    z_ref[...] = _dot(u, wz_ref[...])
    x_ref[...] = _dot(u, wx_ref[...])
    dt_ref[...] = _dot(u, wdt_ref[...])


def _ssm_in(h, g, wz, wx, wdt, *, tm):
    t, d = h.shape
    nz, nx = wz.shape[1], wx.shape[1]
    return pl.pallas_call(
        _ssm_in_body,
        out_shape=(jax.ShapeDtypeStruct((t, nz), F32),
                   jax.ShapeDtypeStruct((t, nx), F32),
                   jax.ShapeDtypeStruct((t, LANES), F32)),
        grid=(t // tm,),
        in_specs=[
            pl.BlockSpec((tm, d), lambda i: (i, 0)),
            _resident((1, d)),
            _resident((d, nz)),
            _resident((d, nx)),
            _resident((d, LANES)),
        ],
        out_specs=(pl.BlockSpec((tm, nz), lambda i: (i, 0)),
                   pl.BlockSpec((tm, nx), lambda i: (i, 0)),
                   pl.BlockSpec((tm, LANES), lambda i: (i, 0))),
        compiler_params=_cparams(1),
        name="ssm_in",
    )(h, g, wz, wx, wdt)


def _ssd_body(z_ref, xbc_ref, dtr_ref, cw_ref, cb_ref, dtb_ref, a_ref, dsk_ref, ng_ref,
              y_ref, state_ref, tail_ref, *, tm):
    i = pl.program_id(1)
    d_inner = SSM_HEADS * HEAD_DIM
    bc_w = SSM_GROUPS * SSM_STATE
    L = SSM_CHUNK

    @pl.when(i == 0)
    def _():
        state_ref[...] = jnp.zeros_like(state_ref)
        tail_ref[...] = jnp.zeros_like(tail_ref)

    cur = xbc_ref[...]
    prev = tail_ref[...]
    row8 = lax.broadcasted_iota(jnp.int32, (8, cur.shape[1]), 0)
    conv = cur * cw_ref[SSM_CONV - 1:SSM_CONV, :] + cb_ref[...]
    for sh in range(1, SSM_CONV):
        cur_s = pltpu.roll(cur, sh, axis=0)
        prev_s = pltpu.roll(prev, sh, axis=0)
        top = jnp.where(row8 < sh, prev_s, cur_s[0:8, :])
        shifted = jnp.concatenate([top, cur_s[8:, :]], axis=0)
        conv = conv + shifted * cw_ref[SSM_CONV - 1 - sh:SSM_CONV - sh, :]
    tail_ref[...] = cur[tm - 8:tm, :]
    xbc = conv * _sigmoid(conv)

    a_neg = -jnp.exp(a_ref[...])
    dt_all = _softplus(dtr_ref[...] + dtb_ref[...])
    tri = _lower_tri(L)
    tri_f = tri.astype(F32)
    lo = _lane_lo((L, LANES))
    lo_row = _lane_lo((1, LANES))
    heads_per_group = SSM_HEADS // SSM_GROUPS

    for c in range(tm // L):
        rows = slice(c * L, (c + 1) * L)
        dt = dt_all[rows, :]
        a_cs = _dot_f32(tri_f, dt * a_neg)
        dt_t = dt.T
        a_cs_t = a_cs.T
        y_tiles = []
        for g in range(SSM_GROUPS):
            b_g = xbc[rows, d_inner + g * SSM_STATE:d_inner + (g + 1) * SSM_STATE]
            c_g = xbc[rows, d_inner + bc_w + g * SSM_STATE:d_inner + bc_w + (g + 1) * SSM_STATE]
            b_bf = b_g.astype(BF16)
            cb = _dot_nt(c_g.astype(BF16), b_bf)
            b_t = b_g.T
            for pr in range(heads_per_group // 2):
                hp = g * (heads_per_group // 2) + pr
                x_pair = xbc[rows, hp * LANES:(hp + 1) * LANES]
                xb = x_pair.astype(BF16)
                st = state_ref[hp]
                lhs_y, lhs_s, cds = [], [], []
                for j in range(2):
                    h = 2 * hp + j
                    a_col = jnp.broadcast_to(a_cs[:, h:h + 1], (L, L))
                    a_row = a_cs_t[h:h + 1, :]
                    dt_row = dt_t[h:h + 1, :]
                    a_last = a_cs_t[h:h + 1, L - 1:L]
                    decay = jnp.exp(jnp.where(tri, a_col - a_row, -jnp.inf))
                    m_diag = cb * decay * dt_row
                    c_off = c_g * jnp.exp(a_col)
                    lhs_y.append(jnp.concatenate([m_diag, c_off], axis=1).astype(BF16))
                    w_row = dt_row * jnp.exp(a_last - a_row)
                    lhs_s.append((b_t * w_row).astype(BF16))
                    cds.append(jnp.exp(a_last))
                rhs = jnp.concatenate([xb, st.astype(BF16)], axis=0)
                y2 = _dot(jnp.concatenate(lhs_y, axis=0), rhs)
                y_tiles.append(jnp.where(lo, y2[0:L, :], y2[L:2 * L, :])
                               + dsk_ref[:, hp * LANES:(hp + 1) * LANES] * x_pair)
                s2 = _dot(jnp.concatenate(lhs_s, axis=0), xb)
                cd = jnp.where(lo_row, cds[0], cds[1])
                state_ref[hp] = st * cd + jnp.where(lo, s2[0:SSM_STATE, :], s2[SSM_STATE:, :])
        y = jnp.concatenate(y_tiles, axis=1)
        zc = z_ref[rows, :]
        y = y * (zc * _sigmoid(zc))
        gw = d_inner // SSM_GROUPS
        normed = []
        for g in range(SSM_GROUPS):
            yg = y[:, g * gw:(g + 1) * gw]
            ms = jnp.mean(yg * yg, axis=-1, keepdims=True)
            normed.append(yg * lax.rsqrt(ms + RMS_EPS))
        y_ref[rows, :] = (jnp.concatenate(normed, axis=1) * ng_ref[...]).astype(BF16)


def _ssd(z, xbc, dtr, cw, cb, dtb, a_log, dsk, ng, *, tm):
    b, s, d_inner = z.shape
    conv_w = xbc.shape[2]
    return pl.pallas_call(
        functools.partial(_ssd_body, tm=tm),
        out_shape=jax.ShapeDtypeStruct((b, s, d_inner), BF16),
        grid=(b, s // tm),
        in_specs=[
            pl.BlockSpec((None, tm, d_inner), lambda bi, i: (bi, i, 0)),
            pl.BlockSpec((None, tm, conv_w), lambda bi, i: (bi, i, 0)),
            pl.BlockSpec((None, tm, LANES), lambda bi, i: (bi, i, 0)),
            _resident((SSM_CONV, conv_w)),
            _resident((1, conv_w)),
            _resident((1, LANES)),
            _resident((1, LANES)),
            _resident((1, d_inner)),
            _resident((1, d_inner)),
        ],
        out_specs=pl.BlockSpec((None, tm, d_inner), lambda bi, i: (bi, i, 0)),
        scratch_shapes=[pltpu.VMEM((SSM_HEADS // 2, SSM_STATE, LANES), F32),
                        pltpu.VMEM((8, conv_w), F32)],
        compiler_params=_cparams(2, sequential=True),
        name="ssd",
    )(z, xbc, dtr, cw, cb, dtb, a_log, dsk, ng)


def _pad_lanes(v):
    v = v.reshape(1, -1).astype(F32)
    return jnp.pad(v, ((0, 0), (0, LANES - v.shape[1])))


def _swap_heads(w):
    return jnp.concatenate([w[:, HEAD_DIM:], w[:, :HEAD_DIM]], axis=1)


def _attention_layer(h3, g, w_in, forget_bias, swa_qn, swa_kn, sinks, fox_qn, fox_kn, w_out,
                     *, tm_in, tq, tm_out):
    b, s, d = h3.shape
    qa_w = SWA_Q_HEADS * HEAD_DIM
    kv_w = SWA_KV_HEADS * HEAD_DIM
    fox_w = FOX_HEADS * HEAD_DIM
    o = 0
    w_qa = w_in[:, o:o + qa_w]; o += qa_w
    w_ka = w_in[:, o:o + kv_w]; o += kv_w
    w_va = w_in[:, o:o + kv_w]; o += kv_w
    w_fox = w_in[:, o:o + 3 * fox_w]; o += 3 * fox_w
    w_f = w_in[:, o:]
    w_main = jnp.concatenate(
        [w_qa, w_ka, _swap_heads(w_ka), w_va, _swap_heads(w_va), w_fox], axis=1).astype(BF16)
    w_fp = jnp.pad(w_f, ((0, 0), (0, LANES - w_f.shape[1]))).astype(BF16)
    gains = jnp.stack([jnp.tile(v.astype(F32), 2) for v in (swa_qn, swa_kn, fox_qn, fox_kn)])
    qkv, cum = _attn_in(h3, g.reshape(1, d), w_main, w_fp, _pad_lanes(forget_bias), gains, tm=tm_in)
    o_a = _swa(sinks.astype(F32), qkv)
    nq = s // tq
    cum_row = cum[:, :, :FOX_HEADS].reshape(b, nq, tq, FOX_HEADS // 2, 2).transpose(0, 3, 1, 4, 2)
    o_b = _fox(qkv, cum, cum_row, tq=tq)
    w_out_b = w_out.astype(BF16)
    h2 = _attn_out(h3.reshape(b * s, d), o_a.reshape(b * s, qa_w), o_b.reshape(b * s, fox_w),
                   w_out_b[:qa_w], w_out_b[qa_w:], tm=tm_out)
    return h2.reshape(b, s, d)


def _ssm_layer(h3, g, w_in, conv_w, conv_b, dt_bias, a_log, d_skip, norm_g, w_out,
               *, tm_in, tm_ssd, tm_out):
    b, s, d = h3.shape
    d_inner = SSM_HEADS * HEAD_DIM
    cw = d_inner + 2 * SSM_GROUPS * SSM_STATE
    w_z = w_in[:, :d_inner].astype(BF16)
    w_x = w_in[:, d_inner:d_inner + cw].astype(BF16)
    w_dt = w_in[:, d_inner + cw:]
    w_dt = jnp.pad(w_dt, ((0, 0), (0, LANES - w_dt.shape[1]))).astype(BF16)
    h2 = h3.reshape(b * s, d)
    z, xbc, dtr = _ssm_in(h2, g.reshape(1, d), w_z, w_x, w_dt, tm=tm_in)
    dsk = jnp.repeat(d_skip.astype(F32), HEAD_DIM).reshape(1, d_inner)
    y = _ssd(z.reshape(b, s, d_inner), xbc.reshape(b, s, cw), dtr.reshape(b, s, LANES),
             conv_w.astype(F32), conv_b.reshape(1, cw).astype(F32), _pad_lanes(dt_bias),
             _pad_lanes(a_log), dsk, norm_g.reshape(1, d_inner).astype(F32), tm=tm_ssd)
    out = _ssm_out(h2, y.reshape(b * s, d_inner), w_out.astype(BF16), tm=tm_out)
    return out.reshape(b, s, d)


def kernel(x, ffn1_norm, ffn1_w_gate, ffn1_w_up, ffn1_w_down, mix_norm, ffn2_norm, ffn2_w_gate, ffn2_w_up, ffn2_w_down, attn_w_in, attn_forget_bias, swa_q_norm, swa_k_norm, swa_sinks, fox_q_norm, fox_k_norm, attn_w_out, ssm_w_in, ssm_conv_w, ssm_conv_b, ssm_dt_bias, ssm_a_log, ssm_d_skip, ssm_norm, ssm_w_out):
    b, s, d = x.shape
    depth = ffn1_norm.shape[0]
    t = b * s
    tm = min(512, t)
    tq = min(256, s)

    def ffn(h3, g, wg, wu, wd):
        out = _ffn(h3.reshape(t, d), g.reshape(1, d), wg.astype(BF16), wu.astype(BF16),
                   wd.astype(BF16), tm=tm)
        return out.reshape(b, s, d)

    h = x
    for layer in range(depth):
        h = ffn(h, ffn1_norm[layer], ffn1_w_gate[layer], ffn1_w_up[layer], ffn1_w_down[layer])
        i = layer // 2
        if layer % 2 == 0:
            h = _attention_layer(h, mix_norm[layer], attn_w_in[i], attn_forget_bias[i],
                                 swa_q_norm[i], swa_k_norm[i], swa_sinks[i],
                                 fox_q_norm[i], fox_k_norm[i], attn_w_out[i],
                                 tm_in=min(512, s), tq=tq, tm_out=tm)
        else:
            h = _ssm_layer(h, mix_norm[layer], ssm_w_in[i], ssm_conv_w[i], ssm_conv_b[i],
                           ssm_dt_bias[i], ssm_a_log[i], ssm_d_skip[i], ssm_norm[i],
                           ssm_w_out[i], tm_in=min(256, t), tm_ssd=min(256, s), tm_out=tm)
        h = ffn(h, ffn2_norm[layer], ffn2_w_gate[layer], ffn2_w_up[layer], ffn2_w_down[layer])
    return h
```

```python
import functools
import math

import jax
import jax.numpy as jnp
from jax import lax
from jax.experimental import pallas as pl
from jax.experimental.pallas import tpu as pltpu

F32 = jnp.float32
BF16 = jnp.bfloat16

RMS_EPS = 1e-6
HEAD_DIM = 64
LANES = 128
SWA_Q_HEADS = 8
SWA_KV_HEADS = 2
SWA_BLOCK = 128
FOX_HEADS = 8
SSM_HEADS = 32
SSM_GROUPS = 4
SSM_STATE = 128
SSM_CHUNK = 128
SSM_CONV = 4
NEG_BIG = -1e30
LOG2E = math.log2(math.e)

VMEM_LIMIT = 56 * 1024 * 1024


def _cparams(n_axes, sequential=False):
    sem = ("arbitrary",) * n_axes if sequential else ("parallel",) * n_axes
    return pltpu.CompilerParams(dimension_semantics=sem, vmem_limit_bytes=VMEM_LIMIT)


def _resident(shape):
    nd = len(shape)
    return pl.BlockSpec(shape, lambda *_: (0,) * nd, pipeline_mode=pl.Buffered(1))


def _rmsnorm(x, g):
    ms = jnp.mean(x * x, axis=-1, keepdims=True)
    return x * lax.rsqrt(ms + RMS_EPS) * g


def _sigmoid(x):
    return 1.0 / (1.0 + jnp.exp(-x))


def _softplus(x):
    return jnp.maximum(x, 0.0) + jnp.log1p(jnp.exp(-jnp.abs(x)))


def _lane_lo(shape):
    return lax.broadcasted_iota(jnp.int32, shape, len(shape) - 1) < HEAD_DIM


def _head_rmsnorm(x, g):
    lo = _lane_lo(x.shape)
    sq = x * x
    s_lo = jnp.sum(jnp.where(lo, sq, 0.0), axis=-1, keepdims=True)
    s_hi = jnp.sum(jnp.where(lo, 0.0, sq), axis=-1, keepdims=True)
    ms = jnp.where(lo, s_lo, s_hi) * (1.0 / HEAD_DIM)
    return x * lax.rsqrt(ms + RMS_EPS) * g


def _dot(a, b):
    return jnp.dot(a, b, preferred_element_type=F32)


def _dot_nt(a, b):
    return lax.dot_general(a, b, (((1,), (1,)), ((), ())), preferred_element_type=F32)


def _dot_f32(a, b):
    return jnp.dot(a, b, preferred_element_type=F32, precision=lax.Precision.HIGHEST)


def _lower_tri(n):
    r = lax.broadcasted_iota(jnp.int32, (n, n), 0)
    c = lax.broadcasted_iota(jnp.int32, (n, n), 1)
    return r >= c


def _ffn_body(h_ref, g_ref, wg_ref, wu_ref, wd_ref, o_ref):
    x = h_ref[...]
    xn = _rmsnorm(x, g_ref[...]).astype(BF16)
    gate = _dot(xn, wg_ref[...])
    up = _dot(xn, wu_ref[...])
    act = (gate * _sigmoid(gate) * up).astype(BF16)
    o_ref[...] = x + 0.5 * _dot(act, wd_ref[...])


def _ffn(h, g, wg, wu, wd, *, tm):
    t, d = h.shape
    dff = wg.shape[1]
    return pl.pallas_call(
        _ffn_body,
        out_shape=jax.ShapeDtypeStruct((t, d), F32),
        grid=(t // tm,),
        in_specs=[
            pl.BlockSpec((tm, d), lambda i: (i, 0)),
            _resident((1, d)),
            _resident((d, dff)),
            _resident((d, dff)),
            _resident((dff, d)),
        ],
        out_specs=pl.BlockSpec((tm, d), lambda i: (i, 0)),
        compiler_params=_cparams(1),
        name="ffn",
    )(h, g, wg, wu, wd)


ATT_NORM_Q_SWA = (0, 1, 2, 3)
ATT_NORM_K_SWA = (4, 5)
ATT_COPY = (6, 7, 16, 17, 18, 19)
ATT_NORM_Q_FOX = (8, 9, 10, 11)
ATT_NORM_K_FOX = (12, 13, 14, 15)
ATT_TILES = 20


def _attn_in_body(h_ref, g_ref, w_ref, wf_ref, fb_ref, gains_ref, qkv_ref, cum_ref, carry_ref):
    i = pl.program_id(1)

    @pl.when(i == 0)
    def _():
        carry_ref[...] = jnp.zeros_like(carry_ref)

    x = h_ref[...]
    u = _rmsnorm(x, g_ref[...]).astype(BF16)
    proj = _dot(u, w_ref[...])
    scale = HEAD_DIM ** -0.5

    def tile(k):
        return proj[:, k * LANES:(k + 1) * LANES]

    def put(k, v):
        qkv_ref[:, k * LANES:(k + 1) * LANES] = v.astype(BF16)

    for k in ATT_NORM_Q_SWA:
        put(k, _head_rmsnorm(tile(k), gains_ref[0:1, :]) * scale)
    for k in ATT_NORM_K_SWA:
        put(k, _head_rmsnorm(tile(k), gains_ref[1:2, :]))
    for k in ATT_NORM_Q_FOX:
        put(k, _head_rmsnorm(tile(k), gains_ref[2:3, :]) * (scale * LOG2E))
    for k in ATT_NORM_K_FOX:
        put(k, _head_rmsnorm(tile(k), gains_ref[3:4, :]))
    for k in ATT_COPY:
        put(k, tile(k))

    fl = _dot(u, wf_ref[...]) + fb_ref[...]
    log_f = -_softplus(-fl) * LOG2E
    tm = log_f.shape[0]
    tri = _lower_tri(tm).astype(F32)
    cum = _dot_f32(tri, log_f) + carry_ref[0:1, :]
    cum_ref[...] = cum
    carry_ref[0:1, :] = cum[tm - 1:tm, :]


def _attn_in(h, g, w, wf, fb, gains, *, tm):
    b, s, d = h.shape
    n = w.shape[1]
    return pl.pallas_call(
        _attn_in_body,
        out_shape=(jax.ShapeDtypeStruct((b, s, n), BF16),
                   jax.ShapeDtypeStruct((b, s, LANES), F32)),
        grid=(b, s // tm),
        in_specs=[
            pl.BlockSpec((None, tm, d), lambda bi, i: (bi, i, 0)),
            _resident((1, d)),
            _resident((d, n)),
            _resident((d, LANES)),
            _resident((1, LANES)),
            _resident((4, LANES)),
        ],
        out_specs=(pl.BlockSpec((None, tm, n), lambda bi, i: (bi, i, 0)),
                   pl.BlockSpec((None, tm, LANES), lambda bi, i: (bi, i, 0))),
        scratch_shapes=[pltpu.VMEM((8, LANES), F32)],
        compiler_params=_cparams(2, sequential=True),
        name="attn_in",
    )(h, g, w, wf, fb, gains)


def _swa_body(sinks_ref, q_ref, kvc_ref, kvp_ref, o_ref):
    n = pl.program_id(1)
    blk = SWA_BLOCK
    q = q_ref[...]
    kv = jnp.concatenate([kvp_ref[...], kvc_ref[...]], axis=0)
    qi = lax.broadcasted_iota(jnp.int32, (blk, 2 * blk), 0) + blk
    ki = lax.broadcasted_iota(jnp.int32, (blk, 2 * blk), 1)
    rel = qi - ki
    mask = (rel >= 0) & (rel < blk) & ((ki >= blk) | (n > 0))
    lo = _lane_lo((blk, LANES))
    group = SWA_Q_HEADS // SWA_KV_HEADS
    for t in range(SWA_Q_HEADS // 2):
        qt = q[:, t * LANES:(t + 1) * LANES]
        outs = []
        for j in range(2):
            head = 2 * t + j
            kv_head = head // group
            var = 0 if kv_head == j else 1
            kk = kv[:, var * LANES:(var + 1) * LANES]
            vv = kv[:, (2 + var) * LANES:(3 + var) * LANES]
            qm = jnp.where(lo if j == 0 else ~lo, qt, jnp.zeros_like(qt))
            sc = _dot_nt(qm, kk)
            sc = jnp.where(mask, sc, -jnp.inf)
            sink = sinks_ref[head]
            m = jnp.maximum(jnp.max(sc, axis=-1, keepdims=True), sink)
            p = jnp.exp(sc - m)
            denom = jnp.sum(p, axis=-1, keepdims=True) + jnp.exp(sink - m)
            outs.append(_dot(p.astype(BF16), vv) / denom)
        o_ref[:, t * LANES:(t + 1) * LANES] = jnp.where(lo, outs[0], outs[1]).astype(BF16)


def _swa(sinks, qkv):
    b, s, _ = qkv.shape
    nb = s // SWA_BLOCK
    qw = SWA_Q_HEADS * HEAD_DIM
    return pl.pallas_call(
        _swa_body,
        out_shape=jax.ShapeDtypeStruct((b, s, qw), BF16),
        grid_spec=pltpu.PrefetchScalarGridSpec(
            num_scalar_prefetch=1,
            grid=(b, nb),
            in_specs=[
                pl.BlockSpec((None, SWA_BLOCK, qw), lambda bi, i, sk: (bi, i, 0)),
                pl.BlockSpec((None, SWA_BLOCK, qw), lambda bi, i, sk: (bi, i, 1)),
                pl.BlockSpec((None, SWA_BLOCK, qw), lambda bi, i, sk: (bi, jnp.maximum(i - 1, 0), 1)),
            ],
            out_specs=pl.BlockSpec((None, SWA_BLOCK, qw), lambda bi, i, sk: (bi, i, 0)),
        ),
        compiler_params=_cparams(2),
        name="swa",
    )(sinks, qkv, qkv, qkv)


FOX_Q_TILE0 = 8
FOX_K_TILE0 = 12
FOX_V_TILE0 = 16


def _fox_body(q_ref, k_ref, v_ref, fq_ref, fk_ref, o_ref, m_ref, acc_ref, sa_ref, sb_ref, *, tq):
    hp = pl.program_id(1)
    i = pl.program_id(2)
    q = q_ref[...]
    fq_all = fq_ref[...]
    lo = _lane_lo((tq, LANES))
    lane = lax.broadcasted_iota(jnp.int32, (tq, LANES), 1)
    zero = jnp.zeros_like(q)
    q2 = jnp.concatenate([jnp.where(lo, q, zero), jnp.where(lo, zero, q)], axis=0)
    fq0 = jnp.sum(jnp.where(lane == 2 * hp, fq_all, 0.0), axis=-1, keepdims=True)
    fq1 = jnp.sum(jnp.where(lane == 2 * hp + 1, fq_all, 0.0), axis=-1, keepdims=True)
    ones = jnp.ones((tq, LANES), BF16)

    m_ref[...] = jnp.full(m_ref.shape, NEG_BIG, F32)
    acc_ref[...] = jnp.zeros(acc_ref.shape, F32)

    def scores(jb, masked):
        start = pl.multiple_of(jb * tq, tq)
        kb = k_ref[pl.ds(start, tq), :]
        fk = fk_ref[jb]
        sc = _dot_nt(q2, kb) + jnp.concatenate([fq0 - fk[0:1, :], fq1 - fk[1:2, :]], axis=0)
        if masked:
            tri = _lower_tri(tq)
            sc = jnp.where(jnp.concatenate([tri, tri], axis=0), sc, NEG_BIG)
        return sc

    def accumulate(sc, jb):
        start = pl.multiple_of(jb * tq, tq)
        v_aug = jnp.concatenate([v_ref[pl.ds(start, tq), :], ones], axis=1)
        m_old = m_ref[...]
        m_new = jnp.maximum(m_old, jnp.max(sc, axis=-1, keepdims=True))
        alpha = jnp.exp2(m_old - m_new)
        p = jnp.concatenate([jnp.exp2(sc[:, t * LANES:(t + 1) * LANES] - m_new)
                             for t in range(tq // LANES)], axis=1).astype(BF16)
        pv = _dot(p, v_aug)
        acc_ref[:, :LANES] = alpha * acc_ref[:, :LANES] + pv[:, :LANES]
        acc_ref[:, LANES:] = alpha * acc_ref[:, LANES:] + pv[:, LANES:]
        m_ref[...] = m_new

    sa_ref[...] = scores(i, True)

    def visited(n):
        return jnp.where(n == 0, i, n - 1)

    def pair(u, carry):
        sb_ref[...] = scores(2 * u, False)
        accumulate(sa_ref[...], visited(2 * u))
        sa_ref[...] = scores(2 * u + 1, False)
        accumulate(sb_ref[...], 2 * u)
        return carry

    lax.fori_loop(0, i // 2, pair, 0)

    @pl.when(i % 2 == 1)
    def _():
        sb_ref[...] = scores(i - 1, False)
        accumulate(sa_ref[...], visited(i - 1))
        accumulate(sb_ref[...], i - 1)

    @pl.when(i % 2 == 0)
    def _():
        accumulate(sa_ref[...], visited(i))

    acc = acc_ref[...]
    o = acc[:, :LANES] / acc[:, LANES:]
    o_ref[...] = jnp.where(lo, o[:tq, :], o[tq:, :]).astype(BF16)


def _fox(qkv, cum_col, cum_row, *, tq):
    b, s, _ = qkv.shape
    nq = s // tq
    npairs = FOX_HEADS // 2
    return pl.pallas_call(
        functools.partial(_fox_body, tq=tq),
        out_shape=jax.ShapeDtypeStruct((b, s, FOX_HEADS * HEAD_DIM), BF16),
        grid=(b, npairs, nq),
        in_specs=[
            pl.BlockSpec((None, tq, LANES), lambda bi, hp, i: (bi, i, FOX_Q_TILE0 + hp)),
            pl.BlockSpec((None, s, LANES), lambda bi, hp, i: (bi, 0, FOX_K_TILE0 + hp)),
            pl.BlockSpec((None, s, LANES), lambda bi, hp, i: (bi, 0, FOX_V_TILE0 + hp)),
            pl.BlockSpec((None, tq, LANES), lambda bi, hp, i: (bi, i, 0)),
            pl.BlockSpec((None, None, nq, 2, tq), lambda bi, hp, i: (bi, hp, 0, 0, 0)),
        ],
        out_specs=pl.BlockSpec((None, tq, LANES), lambda bi, hp, i: (bi, i, hp)),
        scratch_shapes=[pltpu.VMEM((2 * tq, LANES), F32), pltpu.VMEM((2 * tq, 2 * LANES), F32),
                        pltpu.VMEM((2 * tq, tq), F32), pltpu.VMEM((2 * tq, tq), F32)],
        compiler_params=_cparams(3),
        name="fox",
    )(qkv, qkv, qkv, cum_col, cum_row)


def _out2_body(h_ref, a_ref, b_ref, wa_ref, wb_ref, o_ref):
    o_ref[...] = h_ref[...] + _dot(a_ref[...], wa_ref[...]) + _dot(b_ref[...], wb_ref[...])


def _attn_out(h, oa, ob, wa, wb, *, tm):
    t, d = h.shape
    ka, kb = oa.shape[1], ob.shape[1]
    return pl.pallas_call(
        _out2_body,
        out_shape=jax.ShapeDtypeStruct((t, d), F32),
        grid=(t // tm,),
        in_specs=[
            pl.BlockSpec((tm, d), lambda i: (i, 0)),
            pl.BlockSpec((tm, ka), lambda i: (i, 0)),
            pl.BlockSpec((tm, kb), lambda i: (i, 0)),
            _resident((ka, d)),
            _resident((kb, d)),
        ],
        out_specs=pl.BlockSpec((tm, d), lambda i: (i, 0)),
        compiler_params=_cparams(1),
        name="attn_out",
    )(h, oa, ob, wa, wb)


def _out1_body(h_ref, a_ref, w_ref, o_ref):
    o_ref[...] = h_ref[...] + _dot(a_ref[...], w_ref[...])


def _ssm_out(h, y, w, *, tm):
    t, d = h.shape
    k = y.shape[1]
    return pl.pallas_call(
        _out1_body,
        out_shape=jax.ShapeDtypeStruct((t, d), F32),
        grid=(t // tm,),
        in_specs=[
            pl.BlockSpec((tm, d), lambda i: (i, 0)),
            pl.BlockSpec((tm, k), lambda i: (i, 0)),
            _resident((k, d)),
        ],
        out_specs=pl.BlockSpec((tm, d), lambda i: (i, 0)),
        compiler_params=_cparams(1),
        name="ssm_out",
    )(h, y, w)


def _ssm_in_body(h_ref, g_ref, wz_ref, wx_ref, wdt_ref, z_ref, x_ref, dt_ref):
    u = _rmsnorm(h_ref[...], g_ref[...]).astype(BF16)
    z_ref[...] = _dot(u, wz_ref[...])
    x_ref[...] = _dot(u, wx_ref[...])
    dt_ref[...] = _dot(u, wdt_ref[...])


def _ssm_in(h, g, wz, wx, wdt, *, tm):
    t, d = h.shape
    nz, nx = wz.shape[1], wx.shape[1]
    return pl.pallas_call(
        _ssm_in_body,
        out_shape=(jax.ShapeDtypeStruct((t, nz), F32),
                   jax.ShapeDtypeStruct((t, nx), F32),
                   jax.ShapeDtypeStruct((t, LANES), F32)),
        grid=(t // tm,),
        in_specs=[
            pl.BlockSpec((tm, d), lambda i: (i, 0)),
            _resident((1, d)),
            _resident((d, nz)),
            _resident((d, nx)),
            _resident((d, LANES)),
        ],
        out_specs=(pl.BlockSpec((tm, nz), lambda i: (i, 0)),
                   pl.BlockSpec((tm, nx), lambda i: (i, 0)),
                   pl.BlockSpec((tm, LANES), lambda i: (i, 0))),
        compiler_params=_cparams(1),
        name="ssm_in",
    )(h, g, wz, wx, wdt)


def _ssd_body(z_ref, xbc_ref, dtr_ref, cw_ref, cb_ref, dtb_ref, a_ref, dsk_ref, ng_ref,
              y_ref, state_ref, tail_ref, *, tm):
    i = pl.program_id(1)
    d_inner = SSM_HEADS * HEAD_DIM
    bc_w = SSM_GROUPS * SSM_STATE
    L = SSM_CHUNK

    @pl.when(i == 0)
    def _():
        state_ref[...] = jnp.zeros_like(state_ref)
        tail_ref[...] = jnp.zeros_like(tail_ref)

    cur = xbc_ref[...]
    prev = tail_ref[...]
    row8 = lax.broadcasted_iota(jnp.int32, (8, cur.shape[1]), 0)
    conv = cur * cw_ref[SSM_CONV - 1:SSM_CONV, :] + cb_ref[...]
    for sh in range(1, SSM_CONV):
        cur_s = pltpu.roll(cur, sh, axis=0)
        prev_s = pltpu.roll(prev, sh, axis=0)
        top = jnp.where(row8 < sh, prev_s, cur_s[0:8, :])
        shifted = jnp.concatenate([top, cur_s[8:, :]], axis=0)
        conv = conv + shifted * cw_ref[SSM_CONV - 1 - sh:SSM_CONV - sh, :]
    tail_ref[...] = cur[tm - 8:tm, :]
    xbc = conv * _sigmoid(conv)

    a_neg = -jnp.exp(a_ref[...])
    dt_all = _softplus(dtr_ref[...] + dtb_ref[...])
    tri = _lower_tri(L)
    tri_f = tri.astype(F32)
    lo = _lane_lo((L, LANES))
    lo_row = _lane_lo((1, LANES))
    heads_per_group = SSM_HEADS // SSM_GROUPS

    for c in range(tm // L):
        rows = slice(c * L, (c + 1) * L)
        dt = dt_all[rows, :]
        a_cs = _dot_f32(tri_f, dt * a_neg)
        dt_t = dt.T
        a_cs_t = a_cs.T
        y_tiles = []
        for g in range(SSM_GROUPS):
            b_g = xbc[rows, d_inner + g * SSM_STATE:d_inner + (g + 1) * SSM_STATE]
            c_g = xbc[rows, d_inner + bc_w + g * SSM_STATE:d_inner + bc_w + (g + 1) * SSM_STATE]
            b_bf = b_g.astype(BF16)
            cb = _dot_nt(c_g.astype(BF16), b_bf)
            b_t = b_g.T
            for pr in range(heads_per_group // 2):
                hp = g * (heads_per_group // 2) + pr
                x_pair = xbc[rows, hp * LANES:(hp + 1) * LANES]
                xb = x_pair.astype(BF16)
                st = state_ref[hp]
                lhs_y, lhs_s, cds = [], [], []
                for j in range(2):
                    h = 2 * hp + j
                    a_col = jnp.broadcast_to(a_cs[:, h:h + 1], (L, L))
                    a_row = a_cs_t[h:h + 1, :]
                    dt_row = dt_t[h:h + 1, :]
                    a_last = a_cs_t[h:h + 1, L - 1:L]
                    decay = jnp.exp(jnp.where(tri, a_col - a_row, -jnp.inf))
                    m_diag = cb * decay * dt_row
                    c_off = c_g * jnp.exp(a_col)
                    lhs_y.append(jnp.concatenate([m_diag, c_off], axis=1).astype(BF16))
                    w_row = dt_row * jnp.exp(a_last - a_row)
                    lhs_s.append((b_t * w_row).astype(BF16))
                    cds.append(jnp.exp(a_last))
                rhs = jnp.concatenate([xb, st.astype(BF16)], axis=0)
                y2 = _dot(jnp.concatenate(lhs_y, axis=0), rhs)
                y_tiles.append(jnp.where(lo, y2[0:L, :], y2[L:2 * L, :])
                               + dsk_ref[:, hp * LANES:(hp + 1) * LANES] * x_pair)
                s2 = _dot(jnp.concatenate(lhs_s, axis=0), xb)
                cd = jnp.where(lo_row, cds[0], cds[1])
                state_ref[hp] = st * cd + jnp.where(lo, s2[0:SSM_STATE, :], s2[SSM_STATE:, :])
        y = jnp.concatenate(y_tiles, axis=1)
        zc = z_ref[rows, :]
        y = y * (zc * _sigmoid(zc))
        gw = d_inner // SSM_GROUPS
        normed = []
        for g in range(SSM_GROUPS):
            yg = y[:, g * gw:(g + 1) * gw]
            ms = jnp.mean(yg * yg, axis=-1, keepdims=True)
            normed.append(yg * lax.rsqrt(ms + RMS_EPS))
        y_ref[rows, :] = (jnp.concatenate(normed, axis=1) * ng_ref[...]).astype(BF16)


def _ssd(z, xbc, dtr, cw, cb, dtb, a_log, dsk, ng, *, tm):
    b, s, d_inner = z.shape
    conv_w = xbc.shape[2]
    return pl.pallas_call(
        functools.partial(_ssd_body, tm=tm),
        out_shape=jax.ShapeDtypeStruct((b, s, d_inner), BF16),
        grid=(b, s // tm),
        in_specs=[
            pl.BlockSpec((None, tm, d_inner), lambda bi, i: (bi, i, 0)),
            pl.BlockSpec((None, tm, conv_w), lambda bi, i: (bi, i, 0)),
            pl.BlockSpec((None, tm, LANES), lambda bi, i: (bi, i, 0)),
            _resident((SSM_CONV, conv_w)),
            _resident((1, conv_w)),
            _resident((1, LANES)),
            _resident((1, LANES)),
            _resident((1, d_inner)),
            _resident((1, d_inner)),
        ],
        out_specs=pl.BlockSpec((None, tm, d_inner), lambda bi, i: (bi, i, 0)),
        scratch_shapes=[pltpu.VMEM((SSM_HEADS // 2, SSM_STATE, LANES), F32),
                        pltpu.VMEM((8, conv_w), F32)],
        compiler_params=_cparams(2, sequential=True),
        name="ssd",
    )(z, xbc, dtr, cw, cb, dtb, a_log, dsk, ng)


def _pad_lanes(v):
    v = v.reshape(1, -1).astype(F32)
    return jnp.pad(v, ((0, 0), (0, LANES - v.shape[1])))


def _swap_heads(w):
    return jnp.concatenate([w[:, HEAD_DIM:], w[:, :HEAD_DIM]], axis=1)


def _attention_layer(h3, g, w_in, forget_bias, swa_qn, swa_kn, sinks, fox_qn, fox_kn, w_out,
                     *, tm_in, tq, tm_out):
    b, s, d = h3.shape
    qa_w = SWA_Q_HEADS * HEAD_DIM
    kv_w = SWA_KV_HEADS * HEAD_DIM
    fox_w = FOX_HEADS * HEAD_DIM
    o = 0
    w_qa = w_in[:, o:o + qa_w]; o += qa_w
    w_ka = w_in[:, o:o + kv_w]; o += kv_w
    w_va = w_in[:, o:o + kv_w]; o += kv_w
    w_fox = w_in[:, o:o + 3 * fox_w]; o += 3 * fox_w
    w_f = w_in[:, o:]
    w_main = jnp.concatenate(
        [w_qa, w_ka, _swap_heads(w_ka), w_va, _swap_heads(w_va), w_fox], axis=1).astype(BF16)
    w_fp = jnp.pad(w_f, ((0, 0), (0, LANES - w_f.shape[1]))).astype(BF16)
    gains = jnp.stack([jnp.tile(v.astype(F32), 2) for v in (swa_qn, swa_kn, fox_qn, fox_kn)])
    qkv, cum = _attn_in(h3, g.reshape(1, d), w_main, w_fp, _pad_lanes(forget_bias), gains, tm=tm_in)
    o_a = _swa(sinks.astype(F32), qkv)
    nq = s // tq
    cum_row = cum[:, :, :FOX_HEADS].reshape(b, nq, tq, FOX_HEADS // 2, 2).transpose(0, 3, 1, 4, 2)
    o_b = _fox(qkv, cum, cum_row, tq=tq)
    w_out_b = w_out.astype(BF16)
    h2 = _attn_out(h3.reshape(b * s, d), o_a.reshape(b * s, qa_w), o_b.reshape(b * s, fox_w),
                   w_out_b[:qa_w], w_out_b[qa_w:], tm=tm_out)
    return h2.reshape(b, s, d)


def _ssm_layer(h3, g, w_in, conv_w, conv_b, dt_bias, a_log, d_skip, norm_g, w_out,
               *, tm_in, tm_ssd, tm_out):
    b, s, d = h3.shape
    d_inner = SSM_HEADS * HEAD_DIM
    cw = d_inner + 2 * SSM_GROUPS * SSM_STATE
    w_z = w_in[:, :d_inner].astype(BF16)
    w_x = w_in[:, d_inner:d_inner + cw].astype(BF16)
    w_dt = w_in[:, d_inner + cw:]
    w_dt = jnp.pad(w_dt, ((0, 0), (0, LANES - w_dt.shape[1]))).astype(BF16)
    h2 = h3.reshape(b * s, d)
    z, xbc, dtr = _ssm_in(h2, g.reshape(1, d), w_z, w_x, w_dt, tm=tm_in)
    dsk = jnp.repeat(d_skip.astype(F32), HEAD_DIM).reshape(1, d_inner)
    y = _ssd(z.reshape(b, s, d_inner), xbc.reshape(b, s, cw), dtr.reshape(b, s, LANES),
             conv_w.astype(F32), conv_b.reshape(1, cw).astype(F32), _pad_lanes(dt_bias),
             _pad_lanes(a_log), dsk, norm_g.reshape(1, d_inner).astype(F32), tm=tm_ssd)
    out = _ssm_out(h2, y.reshape(b * s, d_inner), w_out.astype(BF16), tm=tm_out)
    return out.reshape(b, s, d)


def kernel(x, ffn1_norm, ffn1_w_gate, ffn1_w_up, ffn1_w_down, mix_norm, ffn2_norm, ffn2_w_gate, ffn2_w_up, ffn2_w_down, attn_w_in, attn_forget_bias, swa_q_norm, swa_k_norm, swa_sinks, fox_q_norm, fox_k_norm, attn_w_out, ssm_w_in, ssm_conv_w, ssm_conv_b, ssm_dt_bias, ssm_a_log, ssm_d_skip, ssm_norm, ssm_w_out):
    b, s, d = x.shape
    depth = ffn1_norm.shape[0]
    t = b * s
    tm = min(512, t)
    tq = min(512, s)

    def ffn(h3, g, wg, wu, wd):
        out = _ffn(h3.reshape(t, d), g.reshape(1, d), wg.astype(BF16), wu.astype(BF16),
                   wd.astype(BF16), tm=tm)
        return out.reshape(b, s, d)

    h = x
    for layer in range(depth):
        h = ffn(h, ffn1_norm[layer], ffn1_w_gate[layer], ffn1_w_up[layer], ffn1_w_down[layer])
        i = layer // 2
        if layer % 2 == 0:
            h = _attention_layer(h, mix_norm[layer], attn_w_in[i], attn_forget_bias[i],
                                 swa_q_norm[i], swa_k_norm[i], swa_sinks[i],
                                 fox_q_norm[i], fox_k_norm[i], attn_w_out[i],
                                 tm_in=min(512, s), tq=tq, tm_out=tm)
        else:
            h = _ssm_layer(h, mix_norm[layer], ssm_w_in[i], ssm_conv_w[i], ssm_conv_b[i],
                           ssm_dt_bias[i], ssm_a_log[i], ssm_d_skip[i], ssm_norm[i],
                           ssm_w_out[i], tm_in=min(256, t), tm_ssd=min(256, s), tm_out=tm)
        h = ffn(h, ffn2_norm[layer], ffn2_w_gate[layer], ffn2_w_up[layer], ffn2_w_down[layer])
    return h
```

```python
import functools
import math

import jax
import jax.numpy as jnp
from jax import lax
from jax.experimental import pallas as pl
from jax.experimental.pallas import tpu as pltpu

F32 = jnp.float32
BF16 = jnp.bfloat16

RMS_EPS = 1e-6
HEAD_DIM = 64
LANES = 128
SWA_Q_HEADS = 8
SWA_KV_HEADS = 2
SWA_BLOCK = 128
FOX_HEADS = 8
SSM_HEADS = 32
SSM_GROUPS = 4
SSM_STATE = 128
SSM_CHUNK = 128
SSM_CONV = 4
NEG_BIG = -1e30
LOG2E = math.log2(math.e)

VMEM_LIMIT = 56 * 1024 * 1024


def _cparams(n_axes, sequential=False):
    sem = ("arbitrary",) * n_axes if sequential else ("parallel",) * n_axes
    return pltpu.CompilerParams(dimension_semantics=sem, vmem_limit_bytes=VMEM_LIMIT)


def _resident(shape):
    nd = len(shape)
    return pl.BlockSpec(shape, lambda *_: (0,) * nd, pipeline_mode=pl.Buffered(1))


def _rmsnorm(x, g):
    ms = jnp.mean(x * x, axis=-1, keepdims=True)
    return x * lax.rsqrt(ms + RMS_EPS) * g


def _sigmoid(x):
    return 1.0 / (1.0 + jnp.exp(-x))


def _softplus(x):
    return jnp.maximum(x, 0.0) + jnp.log1p(jnp.exp(-jnp.abs(x)))


def _lane_lo(shape):
    return lax.broadcasted_iota(jnp.int32, shape, len(shape) - 1) < HEAD_DIM


def _head_rmsnorm(x, g):
    lo = _lane_lo(x.shape)
    sq = x * x
    s_lo = jnp.sum(jnp.where(lo, sq, 0.0), axis=-1, keepdims=True)
    s_hi = jnp.sum(jnp.where(lo, 0.0, sq), axis=-1, keepdims=True)
    ms = jnp.where(lo, s_lo, s_hi) * (1.0 / HEAD_DIM)
    return x * lax.rsqrt(ms + RMS_EPS) * g


def _dot(a, b):
    return jnp.dot(a, b, preferred_element_type=F32)


def _dot_nt(a, b):
    return lax.dot_general(a, b, (((1,), (1,)), ((), ())), preferred_element_type=F32)


def _lower_tri(n):
    r = lax.broadcasted_iota(jnp.int32, (n, n), 0)
    c = lax.broadcasted_iota(jnp.int32, (n, n), 1)
    return r >= c


def _cumsum_rows(x, tri_bf):
    hi = x.astype(BF16)
    r1 = x - hi.astype(F32)
    mid = r1.astype(BF16)
    lo = (r1 - mid.astype(F32)).astype(BF16)
    return _dot(tri_bf, hi) + _dot(tri_bf, mid) + _dot(tri_bf, lo)


def _ffn_body(h_ref, g_ref, wg_ref, wu_ref, wd_ref, o_ref):
    x = h_ref[...]
    xn = _rmsnorm(x, g_ref[...]).astype(BF16)
    gate = _dot(xn, wg_ref[...])
    up = _dot(xn, wu_ref[...])
    act = (gate * _sigmoid(gate) * up).astype(BF16)
    o_ref[...] = x + 0.5 * _dot(act, wd_ref[...])


def _ffn(h, g, wg, wu, wd, *, tm):
    t, d = h.shape
    dff = wg.shape[1]
    return pl.pallas_call(
        _ffn_body,
        out_shape=jax.ShapeDtypeStruct((t, d), F32),
        grid=(t // tm,),
        in_specs=[
            pl.BlockSpec((tm, d), lambda i: (i, 0)),
            _resident((1, d)),
            _resident((d, dff)),
            _resident((d, dff)),
            _resident((dff, d)),
        ],
        out_specs=pl.BlockSpec((tm, d), lambda i: (i, 0)),
        compiler_params=_cparams(1),
        name="ffn",
    )(h, g, wg, wu, wd)


ATT_NORM_Q_SWA = (0, 1, 2, 3)
ATT_NORM_K_SWA = (4, 5)
ATT_COPY = (6, 7, 16, 17, 18, 19)
ATT_NORM_Q_FOX = (8, 9, 10, 11)
ATT_NORM_K_FOX = (12, 13, 14, 15)
ATT_TILES = 20


def _attn_in_body(h_ref, g_ref, w_ref, wf_ref, fb_ref, gains_ref, qkv_ref, cum_ref, carry_ref):
    i = pl.program_id(1)

    @pl.when(i == 0)
    def _():
        carry_ref[...] = jnp.zeros_like(carry_ref)

    x = h_ref[...]
    u = _rmsnorm(x, g_ref[...]).astype(BF16)
    proj = _dot(u, w_ref[...])
    scale = HEAD_DIM ** -0.5

    def tile(k):
        return proj[:, k * LANES:(k + 1) * LANES]

    def put(k, v):
        qkv_ref[:, k * LANES:(k + 1) * LANES] = v.astype(BF16)

    for k in ATT_NORM_Q_SWA:
        put(k, _head_rmsnorm(tile(k), gains_ref[0:1, :]) * scale)
    for k in ATT_NORM_K_SWA:
        put(k, _head_rmsnorm(tile(k), gains_ref[1:2, :]))
    for k in ATT_NORM_Q_FOX:
        put(k, _head_rmsnorm(tile(k), gains_ref[2:3, :]) * (scale * LOG2E))
    for k in ATT_NORM_K_FOX:
        put(k, _head_rmsnorm(tile(k), gains_ref[3:4, :]))
    for k in ATT_COPY:
        put(k, tile(k))

    fl = _dot(u, wf_ref[...]) + fb_ref[...]
    log_f = -_softplus(-fl) * LOG2E
    tm = log_f.shape[0]
    cum = _cumsum_rows(log_f, _lower_tri(tm).astype(BF16)) + carry_ref[0:1, :]
    cum_ref[...] = cum
    carry_ref[0:1, :] = cum[tm - 1:tm, :]


def _attn_in(h, g, w, wf, fb, gains, *, tm):
    b, s, d = h.shape
    n = w.shape[1]
    return pl.pallas_call(
        _attn_in_body,
        out_shape=(jax.ShapeDtypeStruct((b, s, n), BF16),
                   jax.ShapeDtypeStruct((b, s, LANES), F32)),
        grid=(b, s // tm),
        in_specs=[
            pl.BlockSpec((None, tm, d), lambda bi, i: (bi, i, 0)),
            _resident((1, d)),
            _resident((d, n)),
            _resident((d, LANES)),
            _resident((1, LANES)),
            _resident((4, LANES)),
        ],
        out_specs=(pl.BlockSpec((None, tm, n), lambda bi, i: (bi, i, 0)),
                   pl.BlockSpec((None, tm, LANES), lambda bi, i: (bi, i, 0))),
        scratch_shapes=[pltpu.VMEM((8, LANES), F32)],
        compiler_params=_cparams(2, sequential=True),
        name="attn_in",
    )(h, g, w, wf, fb, gains)


def _swa_body(sinks_ref, q_ref, kvc_ref, kvp_ref, o_ref):
    n = pl.program_id(1)
    blk = SWA_BLOCK
    q = q_ref[...]
    kv = jnp.concatenate([kvp_ref[...], kvc_ref[...]], axis=0)
    qi = lax.broadcasted_iota(jnp.int32, (blk, 2 * blk), 0) + blk
    ki = lax.broadcasted_iota(jnp.int32, (blk, 2 * blk), 1)
    rel = qi - ki
    mask = (rel >= 0) & (rel < blk) & ((ki >= blk) | (n > 0))
    lo = _lane_lo((blk, LANES))
    group = SWA_Q_HEADS // SWA_KV_HEADS
    for t in range(SWA_Q_HEADS // 2):
        qt = q[:, t * LANES:(t + 1) * LANES]
        outs = []
        for j in range(2):
            head = 2 * t + j
            kv_head = head // group
            var = 0 if kv_head == j else 1
            kk = kv[:, var * LANES:(var + 1) * LANES]
            vv = kv[:, (2 + var) * LANES:(3 + var) * LANES]
            qm = jnp.where(lo if j == 0 else ~lo, qt, jnp.zeros_like(qt))
            sc = _dot_nt(qm, kk)
            sc = jnp.where(mask, sc, -jnp.inf)
            sink = sinks_ref[head]
            m = jnp.maximum(jnp.max(sc, axis=-1, keepdims=True), sink)
            p = jnp.exp(sc - m)
            denom = jnp.sum(p, axis=-1, keepdims=True) + jnp.exp(sink - m)
            outs.append(_dot(p.astype(BF16), vv) / denom)
        o_ref[:, t * LANES:(t + 1) * LANES] = jnp.where(lo, outs[0], outs[1]).astype(BF16)


def _swa(sinks, qkv):
    b, s, _ = qkv.shape
    nb = s // SWA_BLOCK
    qw = SWA_Q_HEADS * HEAD_DIM
    return pl.pallas_call(
        _swa_body,
        out_shape=jax.ShapeDtypeStruct((b, s, qw), BF16),
        grid_spec=pltpu.PrefetchScalarGridSpec(
            num_scalar_prefetch=1,
            grid=(b, nb),
            in_specs=[
                pl.BlockSpec((None, SWA_BLOCK, qw), lambda bi, i, sk: (bi, i, 0)),
                pl.BlockSpec((None, SWA_BLOCK, qw), lambda bi, i, sk: (bi, i, 1)),
                pl.BlockSpec((None, SWA_BLOCK, qw), lambda bi, i, sk: (bi, jnp.maximum(i - 1, 0), 1)),
            ],
            out_specs=pl.BlockSpec((None, SWA_BLOCK, qw), lambda bi, i, sk: (bi, i, 0)),
        ),
        compiler_params=_cparams(2),
        name="swa",
    )(sinks, qkv, qkv, qkv)


FOX_Q_TILE0 = 8
FOX_K_TILE0 = 12
FOX_V_TILE0 = 16


def _fox_body(q_ref, k_ref, v_ref, fq_ref, fk_ref, o_ref, m_ref, acc_ref, sa_ref, sb_ref, *, tq):
    hp = pl.program_id(1)
    i = pl.program_id(2)
    q = q_ref[...]
    fq_all = fq_ref[...]
    lo = _lane_lo((tq, LANES))
    lane = lax.broadcasted_iota(jnp.int32, (tq, LANES), 1)
    zero = jnp.zeros_like(q)
    q2 = jnp.concatenate([jnp.where(lo, q, zero), jnp.where(lo, zero, q)], axis=0)
    fq0 = jnp.sum(jnp.where(lane == 2 * hp, fq_all, 0.0), axis=-1, keepdims=True)
    fq1 = jnp.sum(jnp.where(lane == 2 * hp + 1, fq_all, 0.0), axis=-1, keepdims=True)
    ones = jnp.ones((tq, LANES), BF16)

    m_ref[...] = jnp.full(m_ref.shape, NEG_BIG, F32)
    acc_ref[...] = jnp.zeros(acc_ref.shape, F32)

    def scores(jb, masked):
        start = pl.multiple_of(jb * tq, tq)
        kb = k_ref[pl.ds(start, tq), :]
        fk = fk_ref[jb]
        sc = _dot_nt(q2, kb) + jnp.concatenate([fq0 - fk[0:1, :], fq1 - fk[1:2, :]], axis=0)
        if masked:
            tri = _lower_tri(tq)
            sc = jnp.where(jnp.concatenate([tri, tri], axis=0), sc, NEG_BIG)
        return sc

    def accumulate(sc, jb):
        start = pl.multiple_of(jb * tq, tq)
        v_aug = jnp.concatenate([v_ref[pl.ds(start, tq), :], ones], axis=1)
        m_old = m_ref[...]
        m_new = jnp.maximum(m_old, jnp.max(sc, axis=-1, keepdims=True))
        alpha = jnp.exp2(m_old - m_new)
        p = jnp.concatenate([jnp.exp2(sc[:, t * LANES:(t + 1) * LANES] - m_new)
                             for t in range(tq // LANES)], axis=1).astype(BF16)
        pv = _dot(p, v_aug)
        acc_ref[:, :LANES] = alpha * acc_ref[:, :LANES] + pv[:, :LANES]
        acc_ref[:, LANES:] = alpha * acc_ref[:, LANES:] + pv[:, LANES:]
        m_ref[...] = m_new

    sa_ref[...] = scores(i, True)

    def visited(n):
        return jnp.where(n == 0, i, n - 1)

    def pair(u, carry):
        sb_ref[...] = scores(2 * u, False)
        accumulate(sa_ref[...], visited(2 * u))
        sa_ref[...] = scores(2 * u + 1, False)
        accumulate(sb_ref[...], 2 * u)
        return carry

    lax.fori_loop(0, i // 2, pair, 0)

    @pl.when(i % 2 == 1)
    def _():
        sb_ref[...] = scores(i - 1, False)
        accumulate(sa_ref[...], visited(i - 1))
        accumulate(sb_ref[...], i - 1)

    @pl.when(i % 2 == 0)
    def _():
        accumulate(sa_ref[...], visited(i))

    acc = acc_ref[...]
    o = acc[:, :LANES] / acc[:, LANES:]
    o_ref[...] = jnp.where(lo, o[:tq, :], o[tq:, :]).astype(BF16)


def _fox(qkv, cum_col, cum_row, *, tq):
    b, s, _ = qkv.shape
    nq = s // tq
    npairs = FOX_HEADS // 2
    return pl.pallas_call(
        functools.partial(_fox_body, tq=tq),
        out_shape=jax.ShapeDtypeStruct((b, s, FOX_HEADS * HEAD_DIM), BF16),
        grid=(b, npairs, nq),
        in_specs=[
            pl.BlockSpec((None, tq, LANES), lambda bi, hp, i: (bi, i, FOX_Q_TILE0 + hp)),
            pl.BlockSpec((None, s, LANES), lambda bi, hp, i: (bi, 0, FOX_K_TILE0 + hp)),
            pl.BlockSpec((None, s, LANES), lambda bi, hp, i: (bi, 0, FOX_V_TILE0 + hp)),
            pl.BlockSpec((None, tq, LANES), lambda bi, hp, i: (bi, i, 0)),
            pl.BlockSpec((None, None, nq, 2, tq), lambda bi, hp, i: (bi, hp, 0, 0, 0)),
        ],
        out_specs=pl.BlockSpec((None, tq, LANES), lambda bi, hp, i: (bi, i, hp)),
        scratch_shapes=[pltpu.VMEM((2 * tq, LANES), F32), pltpu.VMEM((2 * tq, 2 * LANES), F32),
                        pltpu.VMEM((2 * tq, tq), F32), pltpu.VMEM((2 * tq, tq), F32)],
        compiler_params=_cparams(3),
        name="fox",
    )(qkv, qkv, qkv, cum_col, cum_row)


def _out2_body(h_ref, a_ref, b_ref, wa_ref, wb_ref, o_ref):
    o_ref[...] = h_ref[...] + _dot(a_ref[...], wa_ref[...]) + _dot(b_ref[...], wb_ref[...])


def _attn_out(h, oa, ob, wa, wb, *, tm):
    t, d = h.shape
    ka, kb = oa.shape[1], ob.shape[1]
    return pl.pallas_call(
        _out2_body,
        out_shape=jax.ShapeDtypeStruct((t, d), F32),
        grid=(t // tm,),
        in_specs=[
            pl.BlockSpec((tm, d), lambda i: (i, 0)),
            pl.BlockSpec((tm, ka), lambda i: (i, 0)),
            pl.BlockSpec((tm, kb), lambda i: (i, 0)),
            _resident((ka, d)),
            _resident((kb, d)),
        ],
        out_specs=pl.BlockSpec((tm, d), lambda i: (i, 0)),
        compiler_params=_cparams(1),
        name="attn_out",
    )(h, oa, ob, wa, wb)


def _ssm_project(h_ref, g_ref, wz_ref, wx_ref, wdt_ref, z_buf, x_buf, dt_buf):
    u = _rmsnorm(h_ref[...], g_ref[...]).astype(BF16)
    z_buf[...] = _dot(u, wz_ref[...])
    x_buf[...] = _dot(u, wx_ref[...])
    dt_buf[...] = _dot(u, wdt_ref[...])


def _ssm_mix(hprev_ref, z_buf, x_buf, dt_buf, cw_ref, cb_ref, dtb_ref, a_ref, dsk_ref, ng_ref,
             wout_ref, o_ref, y_buf, state_ref, tail_ref, *, tm):
    d_inner = SSM_HEADS * HEAD_DIM
    bc_w = SSM_GROUPS * SSM_STATE
    L = SSM_CHUNK

    cur = x_buf[...]
    prev = tail_ref[...]
    row8 = lax.broadcasted_iota(jnp.int32, (8, cur.shape[1]), 0)
    conv = cur * cw_ref[SSM_CONV - 1:SSM_CONV, :] + cb_ref[...]
    for sh in range(1, SSM_CONV):
        cur_s = pltpu.roll(cur, sh, axis=0)
        prev_s = pltpu.roll(prev, sh, axis=0)
        top = jnp.where(row8 < sh, prev_s, cur_s[0:8, :])
        shifted = jnp.concatenate([top, cur_s[8:, :]], axis=0)
        conv = conv + shifted * cw_ref[SSM_CONV - 1 - sh:SSM_CONV - sh, :]
    tail_ref[...] = cur[tm - 8:tm, :]
    xbc = conv * _sigmoid(conv)

    a2 = -jnp.exp(a_ref[...]) * LOG2E
    dt_all = _softplus(dt_buf[...] + dtb_ref[...])
    tri = _lower_tri(L)
    tri_bf = tri.astype(BF16)
    lo = _lane_lo((L, LANES))
    lo_row = _lane_lo((1, LANES))
    heads_per_group = SSM_HEADS // SSM_GROUPS

    for c in range(tm // L):
        rows = slice(c * L, (c + 1) * L)
        dt = dt_all[rows, :]
        a_cs = _cumsum_rows(dt * a2, tri_bf)
        a_cs_t = a_cs.T
        r_t = a_cs_t - jnp.log2(dt.T)
        y_tiles = []
        for g in range(SSM_GROUPS):
            b_g = xbc[rows, d_inner + g * SSM_STATE:d_inner + (g + 1) * SSM_STATE]
            c_g = xbc[rows, d_inner + bc_w + g * SSM_STATE:d_inner + bc_w + (g + 1) * SSM_STATE]
            cb = _dot_nt(c_g.astype(BF16), b_g.astype(BF16))
            b_t = b_g.T
            for pr in range(heads_per_group // 2):
                hp = g * (heads_per_group // 2) + pr
                x_pair = xbc[rows, hp * LANES:(hp + 1) * LANES]
                xb = x_pair.astype(BF16)
                st = state_ref[hp]
                lhs_y, lhs_s, cds = [], [], []
                for j in range(2):
                    h = 2 * hp + j
                    a_col = jnp.broadcast_to(a_cs[:, h:h + 1], (L, L))
                    r_row = r_t[h:h + 1, :]
                    a_last = a_cs_t[h:h + 1, L - 1:L]
                    m_diag = cb * jnp.exp2(jnp.where(tri, a_col - r_row, -jnp.inf))
                    c_off = c_g * jnp.exp2(a_col)
                    lhs_y.append(jnp.concatenate([m_diag, c_off], axis=1).astype(BF16))
                    lhs_s.append((b_t * jnp.exp2(a_last - r_row)).astype(BF16))
                    cds.append(jnp.exp2(a_last))
                rhs = jnp.concatenate([xb, st.astype(BF16)], axis=0)
                y2 = _dot(jnp.concatenate(lhs_y, axis=0), rhs)
                y_tiles.append(jnp.where(lo, y2[0:L, :], y2[L:2 * L, :])
                               + dsk_ref[:, hp * LANES:(hp + 1) * LANES] * x_pair)
                s2 = _dot(jnp.concatenate(lhs_s, axis=0), xb)
                cd = jnp.where(lo_row, cds[0], cds[1])
                state_ref[hp] = st * cd + jnp.where(lo, s2[0:SSM_STATE, :], s2[SSM_STATE:, :])
        y = jnp.concatenate(y_tiles, axis=1)
        zc = z_buf[rows, :]
        y = y * (zc * _sigmoid(zc))
        gw = d_inner // SSM_GROUPS
        normed = []
        for g in range(SSM_GROUPS):
            yg = y[:, g * gw:(g + 1) * gw]
            ms = jnp.mean(yg * yg, axis=-1, keepdims=True)
            normed.append(yg * lax.rsqrt(ms + RMS_EPS))
        y_buf[rows, :] = (jnp.concatenate(normed, axis=1) * ng_ref[...]).astype(BF16)

    o_ref[...] = hprev_ref[...] + _dot(y_buf[...], wout_ref[...])


def _ssm_body(h_ref, hprev_ref, g_ref, wz_ref, wx_ref, wdt_ref, cw_ref, cb_ref, dtb_ref, a_ref,
              dsk_ref, ng_ref, wout_ref, o_ref, z0, x0, d0, z1, x1, d1, y_buf, state_ref, tail_ref,
              *, tm, tiles_per_seq):
    t = pl.program_id(0)

    @pl.when(t == 0)
    def _():
        z1[...] = jnp.zeros_like(z1)
        x1[...] = jnp.zeros_like(x1)
        d1[...] = jnp.zeros_like(d1)

    @pl.when((t == 0) | ((t - 1) % tiles_per_seq == 0))
    def _():
        state_ref[...] = jnp.zeros_like(state_ref)
        tail_ref[...] = jnp.zeros_like(tail_ref)

    def run(pbuf, mbuf):
        _ssm_project(h_ref, g_ref, wz_ref, wx_ref, wdt_ref, *pbuf)
        _ssm_mix(hprev_ref, *mbuf, cw_ref, cb_ref, dtb_ref, a_ref, dsk_ref, ng_ref, wout_ref,
                 o_ref, y_buf, state_ref, tail_ref, tm=tm)

    @pl.when(t % 2 == 0)
    def _():
        run((z0, x0, d0), (z1, x1, d1))

    @pl.when(t % 2 == 1)
    def _():
        run((z1, x1, d1), (z0, x0, d0))


def _ssm_fused(h, g, wz, wx, wdt, cw, cb, dtb, a_log, dsk, ng, wout, *, tm, seq):
    t, d = h.shape
    d_inner, conv_w = wz.shape[1], wx.shape[1]
    n_tiles = t // tm
    return pl.pallas_call(
        functools.partial(_ssm_body, tm=tm, tiles_per_seq=seq // tm),
        out_shape=jax.ShapeDtypeStruct((t, d), F32),
        grid=(n_tiles + 1,),
        in_specs=[
            pl.BlockSpec((tm, d), lambda i: (jnp.minimum(i, n_tiles - 1), 0)),
            pl.BlockSpec((tm, d), lambda i: (jnp.maximum(i - 1, 0), 0)),
            _resident((1, d)),
            _resident((d, d_inner)),
            _resident((d, conv_w)),
            _resident((d, LANES)),
            _resident((SSM_CONV, conv_w)),
            _resident((1, conv_w)),
            _resident((1, LANES)),
            _resident((1, LANES)),
            _resident((1, d_inner)),
            _resident((1, d_inner)),
            _resident((d_inner, d)),
        ],
        out_specs=pl.BlockSpec((tm, d), lambda i: (jnp.maximum(i - 1, 0), 0)),
        scratch_shapes=[
            pltpu.VMEM((tm, d_inner), F32), pltpu.VMEM((tm, conv_w), F32), pltpu.VMEM((tm, LANES), F32),
            pltpu.VMEM((tm, d_inner), F32), pltpu.VMEM((tm, conv_w), F32), pltpu.VMEM((tm, LANES), F32),
            pltpu.VMEM((tm, d_inner), BF16),
            pltpu.VMEM((SSM_HEADS // 2, SSM_STATE, LANES), F32),
            pltpu.VMEM((8, conv_w), F32),
        ],
        compiler_params=_cparams(1, sequential=True),
        name="ssm",
    )(h, h, g, wz, wx, wdt, cw, cb, dtb, a_log, dsk, ng, wout)


def _pad_lanes(v):
    v = v.reshape(1, -1).astype(F32)
    return jnp.pad(v, ((0, 0), (0, LANES - v.shape[1])))


def _swap_heads(w):
    return jnp.concatenate([w[:, HEAD_DIM:], w[:, :HEAD_DIM]], axis=1)


def _attention_layer(h3, g, w_in, forget_bias, swa_qn, swa_kn, sinks, fox_qn, fox_kn, w_out,
                     *, tm_in, tq, tm_out):
    b, s, d = h3.shape
    qa_w = SWA_Q_HEADS * HEAD_DIM
    kv_w = SWA_KV_HEADS * HEAD_DIM
    fox_w = FOX_HEADS * HEAD_DIM
    o = 0
    w_qa = w_in[:, o:o + qa_w]; o += qa_w
    w_ka = w_in[:, o:o + kv_w]; o += kv_w
    w_va = w_in[:, o:o + kv_w]; o += kv_w
    w_fox = w_in[:, o:o + 3 * fox_w]; o += 3 * fox_w
    w_f = w_in[:, o:]
    w_main = jnp.concatenate(
        [w_qa, w_ka, _swap_heads(w_ka), w_va, _swap_heads(w_va), w_fox], axis=1).astype(BF16)
    w_fp = jnp.pad(w_f, ((0, 0), (0, LANES - w_f.shape[1]))).astype(BF16)
    gains = jnp.stack([jnp.tile(v.astype(F32), 2) for v in (swa_qn, swa_kn, fox_qn, fox_kn)])
    qkv, cum = _attn_in(h3, g.reshape(1, d), w_main, w_fp, _pad_lanes(forget_bias), gains, tm=tm_in)
    o_a = _swa(sinks.astype(F32), qkv)
    nq = s // tq
    cum_row = cum[:, :, :FOX_HEADS].reshape(b, nq, tq, FOX_HEADS // 2, 2).transpose(0, 3, 1, 4, 2)
    o_b = _fox(qkv, cum, cum_row, tq=tq)
    w_out_b = w_out.astype(BF16)
    h2 = _attn_out(h3.reshape(b * s, d), o_a.reshape(b * s, qa_w), o_b.reshape(b * s, fox_w),
                   w_out_b[:qa_w], w_out_b[qa_w:], tm=tm_out)
    return h2.reshape(b, s, d)


def _ssm_layer(h3, g, w_in, conv_w, conv_b, dt_bias, a_log, d_skip, norm_g, w_out,
               *, tm):
    b, s, d = h3.shape
    d_inner = SSM_HEADS * HEAD_DIM
    cw = d_inner + 2 * SSM_GROUPS * SSM_STATE
    w_z = w_in[:, :d_inner].astype(BF16)
    w_x = w_in[:, d_inner:d_inner + cw].astype(BF16)
    w_dt = w_in[:, d_inner + cw:]
    w_dt = jnp.pad(w_dt, ((0, 0), (0, LANES - w_dt.shape[1]))).astype(BF16)
    dsk = jnp.repeat(d_skip.astype(F32), HEAD_DIM).reshape(1, d_inner)
    out = _ssm_fused(h3.reshape(b * s, d), g.reshape(1, d), w_z, w_x, w_dt,
                     conv_w.astype(F32), conv_b.reshape(1, cw).astype(F32), _pad_lanes(dt_bias),
                     _pad_lanes(a_log), dsk, norm_g.reshape(1, d_inner).astype(F32),
                     w_out.astype(BF16), tm=tm, seq=s)
    return out.reshape(b, s, d)


def kernel(x, ffn1_norm, ffn1_w_gate, ffn1_w_up, ffn1_w_down, mix_norm, ffn2_norm, ffn2_w_gate, ffn2_w_up, ffn2_w_down, attn_w_in, attn_forget_bias, swa_q_norm, swa_k_norm, swa_sinks, fox_q_norm, fox_k_norm, attn_w_out, ssm_w_in, ssm_conv_w, ssm_conv_b, ssm_dt_bias, ssm_a_log, ssm_d_skip, ssm_norm, ssm_w_out):
    b, s, d = x.shape
    depth = ffn1_norm.shape[0]
    t = b * s
    tm = min(512, t)
    tq = min(512, s)

    def ffn(h3, g, wg, wu, wd):
        out = _ffn(h3.reshape(t, d), g.reshape(1, d), wg.astype(BF16), wu.astype(BF16),
                   wd.astype(BF16), tm=tm)
        return out.reshape(b, s, d)

    h = x
    for layer in range(depth):
        h = ffn(h, ffn1_norm[layer], ffn1_w_gate[layer], ffn1_w_up[layer], ffn1_w_down[layer])
        i = layer // 2
        if layer % 2 == 0:
            h = _attention_layer(h, mix_norm[layer], attn_w_in[i], attn_forget_bias[i],
                                 swa_q_norm[i], swa_k_norm[i], swa_sinks[i],
                                 fox_q_norm[i], fox_k_norm[i], attn_w_out[i],
                                 tm_in=min(512, s), tq=tq, tm_out=tm)
        else:
            h = _ssm_layer(h, mix_norm[layer], ssm_w_in[i], ssm_conv_w[i], ssm_conv_b[i],
                           ssm_dt_bias[i], ssm_a_log[i], ssm_d_skip[i], ssm_norm[i],
                           ssm_w_out[i], tm=min(256, s))
        h = ffn(h, ffn2_norm[layer], ffn2_w_gate[layer], ffn2_w_up[layer], ffn2_w_down[layer])
    return h
```

```python
import functools
import math

import jax
import jax.numpy as jnp
from jax import lax
from jax.experimental import pallas as pl
from jax.experimental.pallas import tpu as pltpu

F32 = jnp.float32
BF16 = jnp.bfloat16

RMS_EPS = 1e-6
HEAD_DIM = 64
LANES = 128
SWA_Q_HEADS = 8
SWA_KV_HEADS = 2
SWA_BLOCK = 128
FOX_HEADS = 8
SSM_HEADS = 32
SSM_GROUPS = 4
SSM_STATE = 128
SSM_CHUNK = 128
SSM_CONV = 4
NEG_BIG = -1e30
LOG2E = math.log2(math.e)

VMEM_LIMIT = 56 * 1024 * 1024


def _cparams(n_axes, sequential=False):
    sem = ("arbitrary",) * n_axes if sequential else ("parallel",) * n_axes
    return pltpu.CompilerParams(dimension_semantics=sem, vmem_limit_bytes=VMEM_LIMIT)


def _resident(shape):
    nd = len(shape)
    return pl.BlockSpec(shape, lambda *_: (0,) * nd, pipeline_mode=pl.Buffered(1))


def _rmsnorm(x, g):
    ms = jnp.mean(x * x, axis=-1, keepdims=True)
    return x * lax.rsqrt(ms + RMS_EPS) * g


def _sigmoid(x):
    return 1.0 / (1.0 + jnp.exp(-x))


def _softplus(x):
    return jnp.maximum(x, 0.0) + jnp.log1p(jnp.exp(-jnp.abs(x)))


def _lane_lo(shape):
    return lax.broadcasted_iota(jnp.int32, shape, len(shape) - 1) < HEAD_DIM


def _head_rmsnorm(x, g):
    lo = _lane_lo(x.shape)
    sq = x * x
    s_lo = jnp.sum(jnp.where(lo, sq, 0.0), axis=-1, keepdims=True)
    s_hi = jnp.sum(jnp.where(lo, 0.0, sq), axis=-1, keepdims=True)
    ms = jnp.where(lo, s_lo, s_hi) * (1.0 / HEAD_DIM)
    return x * lax.rsqrt(ms + RMS_EPS) * g


def _dot(a, b):
    return jnp.dot(a, b, preferred_element_type=F32)


def _dot_nt(a, b):
    return lax.dot_general(a, b, (((1,), (1,)), ((), ())), preferred_element_type=F32)


def _lower_tri(n):
    r = lax.broadcasted_iota(jnp.int32, (n, n), 0)
    c = lax.broadcasted_iota(jnp.int32, (n, n), 1)
    return r >= c


def _cumsum_rows(x, tri_bf):
    hi = x.astype(BF16)
    r1 = x - hi.astype(F32)
    mid = r1.astype(BF16)
    lo = (r1 - mid.astype(F32)).astype(BF16)
    return _dot(tri_bf, hi) + _dot(tri_bf, mid) + _dot(tri_bf, lo)


def _ffn_body(h_ref, g_ref, wg_ref, wu_ref, wd_ref, o_ref):
    x = h_ref[...]
    xn = _rmsnorm(x, g_ref[...]).astype(BF16)
    gate = _dot(xn, wg_ref[...])
    up = _dot(xn, wu_ref[...])
    act = (gate * _sigmoid(gate) * up).astype(BF16)
    o_ref[...] = x + 0.5 * _dot(act, wd_ref[...])


def _layer_resident(shape, layer):
    nd = len(shape)
    return pl.BlockSpec((None,) + tuple(shape), lambda *_: (layer,) + (0,) * nd,
                        pipeline_mode=pl.Buffered(1))


def _ffn(h, g, wg, wu, wd, *, layer, tm):
    t, d = h.shape
    dff = wg.shape[2]
    return pl.pallas_call(
        _ffn_body,
        out_shape=jax.ShapeDtypeStruct((t, d), F32),
        grid=(t // tm,),
        in_specs=[
            pl.BlockSpec((tm, d), lambda i: (i, 0)),
            _layer_resident((1, d), layer),
            _layer_resident((d, dff), layer),
            _layer_resident((d, dff), layer),
            _layer_resident((dff, d), layer),
        ],
        out_specs=pl.BlockSpec((tm, d), lambda i: (i, 0)),
        compiler_params=_cparams(1),
        name="ffn",
    )(h, g, wg, wu, wd)


ATT_NORM_Q_SWA = (0, 1, 2, 3)
ATT_NORM_K_SWA = (4, 5)
ATT_COPY = (6, 7, 16, 17, 18, 19)
ATT_NORM_Q_FOX = (8, 9, 10, 11)
ATT_NORM_K_FOX = (12, 13, 14, 15)
ATT_TILES = 20


def _attn_in_body(h_ref, g_ref, w_ref, wf_ref, fb_ref, gains_ref, qkv_ref, cum_ref, carry_ref):
    i = pl.program_id(1)

    @pl.when(i == 0)
    def _():
        carry_ref[...] = jnp.zeros_like(carry_ref)

    x = h_ref[...]
    u = _rmsnorm(x, g_ref[...]).astype(BF16)
    proj = _dot(u, w_ref[...])
    scale = HEAD_DIM ** -0.5

    def tile(k):
        return proj[:, k * LANES:(k + 1) * LANES]

    def put(k, v):
        qkv_ref[:, k * LANES:(k + 1) * LANES] = v.astype(BF16)

    for k in ATT_NORM_Q_SWA:
        put(k, _head_rmsnorm(tile(k), gains_ref[0:1, :]) * scale)
    for k in ATT_NORM_K_SWA:
        put(k, _head_rmsnorm(tile(k), gains_ref[1:2, :]))
    for k in ATT_NORM_Q_FOX:
        put(k, _head_rmsnorm(tile(k), gains_ref[2:3, :]) * (scale * LOG2E))
    for k in ATT_NORM_K_FOX:
        put(k, _head_rmsnorm(tile(k), gains_ref[3:4, :]))
    for k in ATT_COPY:
        put(k, tile(k))

    fl = _dot(u, wf_ref[...]) + fb_ref[...]
    log_f = -_softplus(-fl) * LOG2E
    tm = log_f.shape[0]
    cum = _cumsum_rows(log_f, _lower_tri(tm).astype(BF16)) + carry_ref[0:1, :]
    cum_ref[...] = cum
    carry_ref[0:1, :] = cum[tm - 1:tm, :]


def _attn_in(h, g, w, wf, fb, gains, *, tm):
    b, s, d = h.shape
    n = w.shape[1]
    return pl.pallas_call(
        _attn_in_body,
        out_shape=(jax.ShapeDtypeStruct((b, s, n), BF16),
                   jax.ShapeDtypeStruct((b, s, LANES), F32)),
        grid=(b, s // tm),
        in_specs=[
            pl.BlockSpec((None, tm, d), lambda bi, i: (bi, i, 0)),
            _resident((1, d)),
            _resident((d, n)),
            _resident((d, LANES)),
            _resident((1, LANES)),
            _resident((4, LANES)),
        ],
        out_specs=(pl.BlockSpec((None, tm, n), lambda bi, i: (bi, i, 0)),
                   pl.BlockSpec((None, tm, LANES), lambda bi, i: (bi, i, 0))),
        scratch_shapes=[pltpu.VMEM((8, LANES), F32)],
        compiler_params=_cparams(2, sequential=True),
        name="attn_in",
    )(h, g, w, wf, fb, gains)


def _swa_body(sinks_ref, q_ref, kvc_ref, kvp_ref, o_ref):
    n = pl.program_id(1)
    blk = SWA_BLOCK
    q = q_ref[...]
    kv = jnp.concatenate([kvp_ref[...], kvc_ref[...]], axis=0)
    qi = lax.broadcasted_iota(jnp.int32, (blk, 2 * blk), 0) + blk
    ki = lax.broadcasted_iota(jnp.int32, (blk, 2 * blk), 1)
    rel = qi - ki
    mask = (rel >= 0) & (rel < blk) & ((ki >= blk) | (n > 0))
    lo = _lane_lo((blk, LANES))
    group = SWA_Q_HEADS // SWA_KV_HEADS
    for t in range(SWA_Q_HEADS // 2):
        qt = q[:, t * LANES:(t + 1) * LANES]
        outs = []
        for j in range(2):
            head = 2 * t + j
            kv_head = head // group
            var = 0 if kv_head == j else 1
            kk = kv[:, var * LANES:(var + 1) * LANES]
            vv = kv[:, (2 + var) * LANES:(3 + var) * LANES]
            qm = jnp.where(lo if j == 0 else ~lo, qt, jnp.zeros_like(qt))
            sc = _dot_nt(qm, kk)
            sc = jnp.where(mask, sc, -jnp.inf)
            sink = sinks_ref[head]
            m = jnp.maximum(jnp.max(sc, axis=-1, keepdims=True), sink)
            p = jnp.exp(sc - m)
            denom = jnp.sum(p, axis=-1, keepdims=True) + jnp.exp(sink - m)
            outs.append(_dot(p.astype(BF16), vv) / denom)
        o_ref[:, t * LANES:(t + 1) * LANES] = jnp.where(lo, outs[0], outs[1]).astype(BF16)


def _swa(sinks, qkv):
    b, s, _ = qkv.shape
    nb = s // SWA_BLOCK
    qw = SWA_Q_HEADS * HEAD_DIM
    return pl.pallas_call(
        _swa_body,
        out_shape=jax.ShapeDtypeStruct((b, s, qw), BF16),
        grid_spec=pltpu.PrefetchScalarGridSpec(
            num_scalar_prefetch=1,
            grid=(b, nb),
            in_specs=[
                pl.BlockSpec((None, SWA_BLOCK, qw), lambda bi, i, sk: (bi, i, 0)),
                pl.BlockSpec((None, SWA_BLOCK, qw), lambda bi, i, sk: (bi, i, 1)),
                pl.BlockSpec((None, SWA_BLOCK, qw), lambda bi, i, sk: (bi, jnp.maximum(i - 1, 0), 1)),
            ],
            out_specs=pl.BlockSpec((None, SWA_BLOCK, qw), lambda bi, i, sk: (bi, i, 0)),
        ),
        compiler_params=_cparams(2),
        name="swa",
    )(sinks, qkv, qkv, qkv)


FOX_Q_TILE0 = 8
FOX_K_TILE0 = 12
FOX_V_TILE0 = 16


def _fox_body(q_ref, k_ref, v_ref, fq_ref, fk_ref, o_ref, m_ref, acc_ref, sa_ref, sb_ref, *, tq):
    hp = pl.program_id(1)
    i = pl.program_id(2)
    q = q_ref[...]
    fq_all = fq_ref[...]
    lo = _lane_lo((tq, LANES))
    lane = lax.broadcasted_iota(jnp.int32, (tq, LANES), 1)
    zero = jnp.zeros_like(q)
    q2 = jnp.concatenate([jnp.where(lo, q, zero), jnp.where(lo, zero, q)], axis=0)
    fq0 = jnp.sum(jnp.where(lane == 2 * hp, fq_all, 0.0), axis=-1, keepdims=True)
    fq1 = jnp.sum(jnp.where(lane == 2 * hp + 1, fq_all, 0.0), axis=-1, keepdims=True)
    ones = jnp.ones((tq, LANES), BF16)

    m_ref[...] = jnp.full(m_ref.shape, NEG_BIG, F32)
    acc_ref[...] = jnp.zeros(acc_ref.shape, F32)

    def scores(jb, masked):
        start = pl.multiple_of(jb * tq, tq)
        kb = k_ref[pl.ds(start, tq), :]
        fk = fk_ref[jb]
        sc = _dot_nt(q2, kb) + jnp.concatenate([fq0 - fk[0:1, :], fq1 - fk[1:2, :]], axis=0)
        if masked:
            tri = _lower_tri(tq)
            sc = jnp.where(jnp.concatenate([tri, tri], axis=0), sc, NEG_BIG)
        return sc

    def accumulate(sc, jb):
        start = pl.multiple_of(jb * tq, tq)
        v_aug = jnp.concatenate([v_ref[pl.ds(start, tq), :], ones], axis=1)
        m_old = m_ref[...]
        m_new = jnp.maximum(m_old, jnp.max(sc, axis=-1, keepdims=True))
        alpha = jnp.exp2(m_old - m_new)
        p = jnp.concatenate([jnp.exp2(sc[:, t * LANES:(t + 1) * LANES] - m_new)
                             for t in range(tq // LANES)], axis=1).astype(BF16)
        pv = _dot(p, v_aug)
        acc_ref[:, :LANES] = alpha * acc_ref[:, :LANES] + pv[:, :LANES]
        acc_ref[:, LANES:] = alpha * acc_ref[:, LANES:] + pv[:, LANES:]
        m_ref[...] = m_new

    sa_ref[...] = scores(i, True)

    def visited(n):
        return jnp.where(n == 0, i, n - 1)

    def pair(u, carry):
        sb_ref[...] = scores(2 * u, False)
        accumulate(sa_ref[...], visited(2 * u))
        sa_ref[...] = scores(2 * u + 1, False)
        accumulate(sb_ref[...], 2 * u)
        return carry

    lax.fori_loop(0, i // 2, pair, 0)

    @pl.when(i % 2 == 1)
    def _():
        sb_ref[...] = scores(i - 1, False)
        accumulate(sa_ref[...], visited(i - 1))
        accumulate(sb_ref[...], i - 1)

    @pl.when(i % 2 == 0)
    def _():
        accumulate(sa_ref[...], visited(i))

    acc = acc_ref[...]
    o = acc[:, :LANES] / acc[:, LANES:]
    o_ref[...] = jnp.where(lo, o[:tq, :], o[tq:, :]).astype(BF16)


def _fox(qkv, cum_col, cum_row, *, tq):
    b, s, _ = qkv.shape
    nq = s // tq
    npairs = FOX_HEADS // 2
    return pl.pallas_call(
        functools.partial(_fox_body, tq=tq),
        out_shape=jax.ShapeDtypeStruct((b, s, FOX_HEADS * HEAD_DIM), BF16),
        grid=(b, npairs, nq),
        in_specs=[
            pl.BlockSpec((None, tq, LANES), lambda bi, hp, i: (bi, i, FOX_Q_TILE0 + hp)),
            pl.BlockSpec((None, s, LANES), lambda bi, hp, i: (bi, 0, FOX_K_TILE0 + hp)),
            pl.BlockSpec((None, s, LANES), lambda bi, hp, i: (bi, 0, FOX_V_TILE0 + hp)),
            pl.BlockSpec((None, tq, LANES), lambda bi, hp, i: (bi, i, 0)),
            pl.BlockSpec((None, None, nq, 2, tq), lambda bi, hp, i: (bi, hp, 0, 0, 0)),
        ],
        out_specs=pl.BlockSpec((None, tq, LANES), lambda bi, hp, i: (bi, i, hp)),
        scratch_shapes=[pltpu.VMEM((2 * tq, LANES), F32), pltpu.VMEM((2 * tq, 2 * LANES), F32),
                        pltpu.VMEM((2 * tq, tq), F32), pltpu.VMEM((2 * tq, tq), F32)],
        compiler_params=_cparams(3),
        name="fox",
    )(qkv, qkv, qkv, cum_col, cum_row)


def _out2_body(h_ref, a_ref, b_ref, wa_ref, wb_ref, o_ref):
    o_ref[...] = h_ref[...] + _dot(a_ref[...], wa_ref[...]) + _dot(b_ref[...], wb_ref[...])


def _attn_out(h, oa, ob, wa, wb, *, tm):
    t, d = h.shape
    ka, kb = oa.shape[1], ob.shape[1]
    return pl.pallas_call(
        _out2_body,
        out_shape=jax.ShapeDtypeStruct((t, d), F32),
        grid=(t // tm,),
        in_specs=[
            pl.BlockSpec((tm, d), lambda i: (i, 0)),
            pl.BlockSpec((tm, ka), lambda i: (i, 0)),
            pl.BlockSpec((tm, kb), lambda i: (i, 0)),
            _resident((ka, d)),
            _resident((kb, d)),
        ],
        out_specs=pl.BlockSpec((tm, d), lambda i: (i, 0)),
        compiler_params=_cparams(1),
        name="attn_out",
    )(h, oa, ob, wa, wb)


SSM_PROJ_BLOCK = 512


def _ssm_project_pieces(h_ref, g_ref, wz_ref, wx_ref, wdt_ref, z_buf, x_buf, dt_buf):
    u = _rmsnorm(h_ref[...], g_ref[...]).astype(BF16)

    def piece(buf, w_ref, c0, c1):
        def run():
            buf[:, c0:c1] = _dot(u, w_ref[:, c0:c1])
        return run

    pieces = [piece(dt_buf, wdt_ref, 0, LANES)]
    for buf, w_ref in ((x_buf, wx_ref), (z_buf, wz_ref)):
        n = w_ref.shape[1]
        pieces += [piece(buf, w_ref, c0, min(c0 + SSM_PROJ_BLOCK, n))
                   for c0 in range(0, n, SSM_PROJ_BLOCK)]
    return pieces


def _ssm_mix(hprev_ref, z_buf, x_buf, dt_buf, cw_ref, cb_ref, dtb_ref, a_ref, dsk_ref, ng_ref,
             wout_ref, o_ref, state_ref, tail_ref, side_work, *, tm):
    d_inner = SSM_HEADS * HEAD_DIM
    bc_w = SSM_GROUPS * SSM_STATE
    L = SSM_CHUNK
    def do_side_work(n):
        for _ in range(n):
            if side_work:
                side_work.pop(0)()

    conv_w = x_buf.shape[1]
    row8 = lax.broadcasted_iota(jnp.int32, (8, SSM_PROJ_BLOCK), 0)
    xbc_blocks = []
    for c0 in range(0, conv_w, SSM_PROJ_BLOCK):
        cols = slice(c0, c0 + SSM_PROJ_BLOCK)
        cur = x_buf[:, cols]
        prev = tail_ref[:, cols]
        conv = cur * cw_ref[SSM_CONV - 1:SSM_CONV, cols] + cb_ref[:, cols]
        for sh in range(1, SSM_CONV):
            cur_s = pltpu.roll(cur, sh, axis=0)
            prev_s = pltpu.roll(prev, sh, axis=0)
            top = jnp.where(row8 < sh, prev_s, cur_s[0:8, :])
            shifted = jnp.concatenate([top, cur_s[8:, :]], axis=0)
            conv = conv + shifted * cw_ref[SSM_CONV - 1 - sh:SSM_CONV - sh, cols]
        tail_ref[:, cols] = cur[tm - 8:tm, :]
        xbc_blocks.append(conv * _sigmoid(conv))
        do_side_work(1)
    xbc = jnp.concatenate(xbc_blocks, axis=1)

    a2 = -jnp.exp(a_ref[...]) * LOG2E
    dt_all = _softplus(dt_buf[...] + dtb_ref[...])
    tri = _lower_tri(L)
    tri_bf = tri.astype(BF16)
    lo = _lane_lo((L, LANES))
    lo_row = _lane_lo((1, LANES))
    heads_per_group = SSM_HEADS // SSM_GROUPS

    for c in range(tm // L):
        rows = slice(c * L, (c + 1) * L)
        dt = dt_all[rows, :]
        a_cs = _cumsum_rows(dt * a2, tri_bf)
        a_cs_t = a_cs.T
        r_t = a_cs_t - jnp.log2(dt.T)
        y_tiles = []
        for g in range(SSM_GROUPS):
            b_g = xbc[rows, d_inner + g * SSM_STATE:d_inner + (g + 1) * SSM_STATE]
            c_g = xbc[rows, d_inner + bc_w + g * SSM_STATE:d_inner + bc_w + (g + 1) * SSM_STATE]
            cb = _dot_nt(c_g.astype(BF16), b_g.astype(BF16))
            b_t = b_g.T
            for pr in range(heads_per_group // 2):
                hp = g * (heads_per_group // 2) + pr
                x_pair = xbc[rows, hp * LANES:(hp + 1) * LANES]
                xb = x_pair.astype(BF16)
                st = state_ref[hp]
                lhs_y, lhs_s, cds = [], [], []
                for j in range(2):
                    h = 2 * hp + j
                    a_col = jnp.broadcast_to(a_cs[:, h:h + 1], (L, L))
                    r_row = r_t[h:h + 1, :]
                    a_last = a_cs_t[h:h + 1, L - 1:L]
                    m_diag = cb * jnp.exp2(jnp.where(tri, a_col - r_row, -jnp.inf))
                    c_off = c_g * jnp.exp2(a_col)
                    lhs_y.append(jnp.concatenate([m_diag, c_off], axis=1).astype(BF16))
                    lhs_s.append((b_t * jnp.exp2(a_last - r_row)).astype(BF16))
                    cds.append(jnp.exp2(a_last))
                rhs = jnp.concatenate([xb, st.astype(BF16)], axis=0)
                y2 = _dot(jnp.concatenate(lhs_y, axis=0), rhs)
                y_tiles.append(jnp.where(lo, y2[0:L, :], y2[L:2 * L, :])
                               + dsk_ref[:, hp * LANES:(hp + 1) * LANES] * x_pair)
                s2 = _dot(jnp.concatenate(lhs_s, axis=0), xb)
                cd = jnp.where(lo_row, cds[0], cds[1])
                state_ref[hp] = st * cd + jnp.where(lo, s2[0:SSM_STATE, :], s2[SSM_STATE:, :])
            do_side_work(1)
        y = jnp.concatenate(y_tiles, axis=1)
        zc = z_buf[rows, :]
        y = y * (zc * _sigmoid(zc))
        gw = d_inner // SSM_GROUPS
        normed = []
        for g in range(SSM_GROUPS):
            yg = y[:, g * gw:(g + 1) * gw]
            ms = jnp.mean(yg * yg, axis=-1, keepdims=True)
            normed.append(yg * lax.rsqrt(ms + RMS_EPS))
        y_out = (jnp.concatenate(normed, axis=1) * ng_ref[...]).astype(BF16)
        o_ref[rows, :] = hprev_ref[rows, :] + _dot(y_out, wout_ref[...])

    do_side_work(len(side_work))


def _ssm_body(h_ref, hprev_ref, g_ref, wz_ref, wx_ref, wdt_ref, cw_ref, cb_ref, dtb_ref, a_ref,
              dsk_ref, ng_ref, wout_ref, o_ref, z0, x0, d0, z1, x1, d1, state_ref, tail_ref,
              *, tm, tiles_per_seq):
    t = pl.program_id(0)

    @pl.when(t == 0)
    def _():
        z1[...] = jnp.zeros_like(z1)
        x1[...] = jnp.zeros_like(x1)
        d1[...] = jnp.zeros_like(d1)

    @pl.when((t == 0) | ((t - 1) % tiles_per_seq == 0))
    def _():
        state_ref[...] = jnp.zeros_like(state_ref)
        tail_ref[...] = jnp.zeros_like(tail_ref)

    def run(pbuf, mbuf):
        pieces = _ssm_project_pieces(h_ref, g_ref, wz_ref, wx_ref, wdt_ref, *pbuf)
        _ssm_mix(hprev_ref, *mbuf, cw_ref, cb_ref, dtb_ref, a_ref, dsk_ref, ng_ref, wout_ref,
                 o_ref, state_ref, tail_ref, pieces, tm=tm)

    @pl.when(t % 2 == 0)
    def _():
        run((z0, x0, d0), (z1, x1, d1))

    @pl.when(t % 2 == 1)
    def _():
        run((z1, x1, d1), (z0, x0, d0))


def _ssm_fused(h, g, wz, wx, wdt, cw, cb, dtb, a_log, dsk, ng, wout, *, tm, seq):
    t, d = h.shape
    d_inner, conv_w = wz.shape[1], wx.shape[1]
    n_tiles = t // tm
    return pl.pallas_call(
        functools.partial(_ssm_body, tm=tm, tiles_per_seq=seq // tm),
        out_shape=jax.ShapeDtypeStruct((t, d), F32),
        grid=(n_tiles + 1,),
        in_specs=[
            pl.BlockSpec((tm, d), lambda i: (jnp.minimum(i, n_tiles - 1), 0)),
            pl.BlockSpec((tm, d), lambda i: (jnp.maximum(i - 1, 0), 0)),
            _resident((1, d)),
            _resident((d, d_inner)),
            _resident((d, conv_w)),
            _resident((d, LANES)),
            _resident((SSM_CONV, conv_w)),
            _resident((1, conv_w)),
            _resident((1, LANES)),
            _resident((1, LANES)),
            _resident((1, d_inner)),
            _resident((1, d_inner)),
            _resident((d_inner, d)),
        ],
        out_specs=pl.BlockSpec((tm, d), lambda i: (jnp.maximum(i - 1, 0), 0)),
        scratch_shapes=[
            pltpu.VMEM((tm, d_inner), F32), pltpu.VMEM((tm, conv_w), F32), pltpu.VMEM((tm, LANES), F32),
            pltpu.VMEM((tm, d_inner), F32), pltpu.VMEM((tm, conv_w), F32), pltpu.VMEM((tm, LANES), F32),
            pltpu.VMEM((SSM_HEADS // 2, SSM_STATE, LANES), F32),
            pltpu.VMEM((8, conv_w), F32),
        ],
        compiler_params=_cparams(1, sequential=True),
        name="ssm",
    )(h, h, g, wz, wx, wdt, cw, cb, dtb, a_log, dsk, ng, wout)


def _pad_lanes(v):
    v = v.reshape(1, -1).astype(F32)
    return jnp.pad(v, ((0, 0), (0, LANES - v.shape[1])))


def _swap_heads(w):
    return jnp.concatenate([w[:, HEAD_DIM:], w[:, :HEAD_DIM]], axis=1)


def _attention_layer(h3, g, w_in, forget_bias, swa_qn, swa_kn, sinks, fox_qn, fox_kn, w_out,
                     *, tm_in, tq, tm_out):
    b, s, d = h3.shape
    qa_w = SWA_Q_HEADS * HEAD_DIM
    kv_w = SWA_KV_HEADS * HEAD_DIM
    fox_w = FOX_HEADS * HEAD_DIM
    o = 0
    w_qa = w_in[:, o:o + qa_w]; o += qa_w
    w_ka = w_in[:, o:o + kv_w]; o += kv_w
    w_va = w_in[:, o:o + kv_w]; o += kv_w
    w_fox = w_in[:, o:o + 3 * fox_w]; o += 3 * fox_w
    w_f = w_in[:, o:]
    w_main = jnp.concatenate(
        [w_qa, w_ka, _swap_heads(w_ka), w_va, _swap_heads(w_va), w_fox], axis=1).astype(BF16)
    w_fp = jnp.pad(w_f, ((0, 0), (0, LANES - w_f.shape[1]))).astype(BF16)
    gains = jnp.stack([jnp.tile(v.astype(F32), 2) for v in (swa_qn, swa_kn, fox_qn, fox_kn)])
    qkv, cum = _attn_in(h3, g.reshape(1, d), w_main, w_fp, _pad_lanes(forget_bias), gains, tm=tm_in)
    o_a = _swa(sinks.astype(F32), qkv)
    nq = s // tq
    cum_row = cum[:, :, :FOX_HEADS].reshape(b, nq, tq, FOX_HEADS // 2, 2).transpose(0, 3, 1, 4, 2)
    o_b = _fox(qkv, cum, cum_row, tq=tq)
    w_out_b = w_out.astype(BF16)
    h2 = _attn_out(h3.reshape(b * s, d), o_a.reshape(b * s, qa_w), o_b.reshape(b * s, fox_w),
                   w_out_b[:qa_w], w_out_b[qa_w:], tm=tm_out)
    return h2.reshape(b, s, d)


def _ssm_layer(h3, g, w_in, conv_w, conv_b, dt_bias, a_log, d_skip, norm_g, w_out,
               *, tm):
    b, s, d = h3.shape
    d_inner = SSM_HEADS * HEAD_DIM
    cw = d_inner + 2 * SSM_GROUPS * SSM_STATE
    w_z = w_in[:, :d_inner].astype(BF16)
    w_x = w_in[:, d_inner:d_inner + cw].astype(BF16)
    w_dt = w_in[:, d_inner + cw:]
    w_dt = jnp.pad(w_dt, ((0, 0), (0, LANES - w_dt.shape[1]))).astype(BF16)
    dsk = jnp.repeat(d_skip.astype(F32), HEAD_DIM).reshape(1, d_inner)
    out = _ssm_fused(h3.reshape(b * s, d), g.reshape(1, d), w_z, w_x, w_dt,
                     conv_w.astype(F32), conv_b.reshape(1, cw).astype(F32), _pad_lanes(dt_bias),
                     _pad_lanes(a_log), dsk, norm_g.reshape(1, d_inner).astype(F32),
                     w_out.astype(BF16), tm=tm, seq=s)
    return out.reshape(b, s, d)


def kernel(x, ffn1_norm, ffn1_w_gate, ffn1_w_up, ffn1_w_down, mix_norm, ffn2_norm, ffn2_w_gate, ffn2_w_up, ffn2_w_down, attn_w_in, attn_forget_bias, swa_q_norm, swa_k_norm, swa_sinks, fox_q_norm, fox_k_norm, attn_w_out, ssm_w_in, ssm_conv_w, ssm_conv_b, ssm_dt_bias, ssm_a_log, ssm_d_skip, ssm_norm, ssm_w_out):
    b, s, d = x.shape
    depth = ffn1_norm.shape[0]
    t = b * s
    tm = min(512, t)
    tq = min(512, s)

    ffn1 = (ffn1_norm.reshape(depth, 1, d), ffn1_w_gate.astype(BF16), ffn1_w_up.astype(BF16),
            ffn1_w_down.astype(BF16))
    ffn2 = (ffn2_norm.reshape(depth, 1, d), ffn2_w_gate.astype(BF16), ffn2_w_up.astype(BF16),
            ffn2_w_down.astype(BF16))

    def ffn(h3, params, layer):
        return _ffn(h3.reshape(t, d), *params, layer=layer, tm=tm).reshape(b, s, d)

    h = x
    for layer in range(depth):
        h = ffn(h, ffn1, layer)
        i = layer // 2
        if layer % 2 == 0:
            h = _attention_layer(h, mix_norm[layer], attn_w_in[i], attn_forget_bias[i],
                                 swa_q_norm[i], swa_k_norm[i], swa_sinks[i],
                                 fox_q_norm[i], fox_k_norm[i], attn_w_out[i],
                                 tm_in=min(512, s), tq=tq, tm_out=tm)
        else:
            h = _ssm_layer(h, mix_norm[layer], ssm_w_in[i], ssm_conv_w[i], ssm_conv_b[i],
                           ssm_dt_bias[i], ssm_a_log[i], ssm_d_skip[i], ssm_norm[i],
                           ssm_w_out[i], tm=min(256, s))
        h = ffn(h, ffn2, layer)
    return h
```

```python
import functools
import math

import jax
import jax.numpy as jnp
from jax import lax
from jax.experimental import pallas as pl
from jax.experimental.pallas import tpu as pltpu

F32 = jnp.float32
BF16 = jnp.bfloat16

RMS_EPS = 1e-6
HEAD_DIM = 64
LANES = 128
SWA_Q_HEADS = 8
SWA_KV_HEADS = 2
SWA_BLOCK = 128
FOX_HEADS = 8
SSM_HEADS = 32
SSM_GROUPS = 4
SSM_STATE = 128
SSM_CHUNK = 128
SSM_CONV = 4
NEG_BIG = -1e30
LOG2E = math.log2(math.e)

VMEM_LIMIT = 56 * 1024 * 1024
VMEM_LIMIT_SSM = 60 * 1024 * 1024


def _cparams(n_axes, sequential=False, vmem_limit=VMEM_LIMIT):
    sem = ("arbitrary",) * n_axes if sequential else ("parallel",) * n_axes
    return pltpu.CompilerParams(dimension_semantics=sem, vmem_limit_bytes=vmem_limit)


def _resident(shape):
    nd = len(shape)
    return pl.BlockSpec(shape, lambda *_: (0,) * nd, pipeline_mode=pl.Buffered(1))


def _rmsnorm(x, g):
    ms = jnp.mean(x * x, axis=-1, keepdims=True)
    return x * lax.rsqrt(ms + RMS_EPS) * g


def _sigmoid(x):
    return 1.0 / (1.0 + jnp.exp(-x))


def _softplus(x):
    return jnp.maximum(x, 0.0) + jnp.log1p(jnp.exp(-jnp.abs(x)))


def _lane_lo(shape):
    return lax.broadcasted_iota(jnp.int32, shape, len(shape) - 1) < HEAD_DIM


def _head_rmsnorm(x, g):
    lo = _lane_lo(x.shape)
    sq = x * x
    s_lo = jnp.sum(jnp.where(lo, sq, 0.0), axis=-1, keepdims=True)
    s_hi = jnp.sum(jnp.where(lo, 0.0, sq), axis=-1, keepdims=True)
    ms = jnp.where(lo, s_lo, s_hi) * (1.0 / HEAD_DIM)
    return x * lax.rsqrt(ms + RMS_EPS) * g


def _dot(a, b):
    return jnp.dot(a, b, preferred_element_type=F32)


def _dot_nt(a, b):
    return lax.dot_general(a, b, (((1,), (1,)), ((), ())), preferred_element_type=F32)


def _lower_tri(n):
    r = lax.broadcasted_iota(jnp.int32, (n, n), 0)
    c = lax.broadcasted_iota(jnp.int32, (n, n), 1)
    return r >= c


def _cumsum_rows(x, tri_bf):
    hi = x.astype(BF16)
    r1 = x - hi.astype(F32)
    mid = r1.astype(BF16)
    lo = (r1 - mid.astype(F32)).astype(BF16)
    return _dot(tri_bf, hi) + _dot(tri_bf, mid) + _dot(tri_bf, lo)


def _ffn_body(*refs, with_mixer_out):
    if with_mixer_out:
        h_ref, a_ref, b_ref, wa_ref, wb_ref, g_ref, wg_ref, wu_ref, wd_ref, o_ref = refs
        x = h_ref[...] + _dot(a_ref[...], wa_ref[...]) + _dot(b_ref[...], wb_ref[...])
    else:
        h_ref, g_ref, wg_ref, wu_ref, wd_ref, o_ref = refs
        x = h_ref[...]
    xn = _rmsnorm(x, g_ref[...]).astype(BF16)
    gate = _dot(xn, wg_ref[...])
    up = _dot(xn, wu_ref[...])
    act = (gate * _sigmoid(gate) * up).astype(BF16)
    o_ref[...] = x + 0.5 * _dot(act, wd_ref[...])


def _layer_resident(shape, layer):
    nd = len(shape)
    return pl.BlockSpec((None,) + tuple(shape), lambda *_: (layer,) + (0,) * nd,
                        pipeline_mode=pl.Buffered(1))


def _ffn(h, g, wg, wu, wd, *, layer, tm, mixer_out=None):
    t, d = h.shape
    dff = wg.shape[2]
    tile = lambda width: pl.BlockSpec((tm, width), lambda i: (i, 0))
    mix_args, mix_specs = (), []
    if mixer_out is not None:
        o_a, o_b, w_a, w_b = mix_args = mixer_out
        mix_specs = [tile(o_a.shape[1]), tile(o_b.shape[1]), _resident(w_a.shape), _resident(w_b.shape)]
    return pl.pallas_call(
        functools.partial(_ffn_body, with_mixer_out=mixer_out is not None),
        out_shape=jax.ShapeDtypeStruct((t, d), F32),
        grid=(t // tm,),
        in_specs=[tile(d)] + mix_specs + [
            _layer_resident((1, d), layer),
            _layer_resident((d, dff), layer),
            _layer_resident((d, dff), layer),
            _layer_resident((dff, d), layer),
        ],
        out_specs=tile(d),
        compiler_params=_cparams(1),
        name="ffn",
    )(h, *mix_args, g, wg, wu, wd)


ATT_NORM_Q_SWA = (0, 1, 2, 3)
ATT_NORM_K_SWA = (4, 5)
ATT_COPY = (6, 7, 16, 17, 18, 19)
ATT_NORM_Q_FOX = (8, 9, 10, 11)
ATT_NORM_K_FOX = (12, 13, 14, 15)
ATT_TILES = 20


def _attn_in_body(h_ref, g_ref, w_ref, wf_ref, fb_ref, gains_ref, qkv_ref, cum_ref, carry_ref):
    i = pl.program_id(1)

    @pl.when(i == 0)
    def _():
        carry_ref[...] = jnp.zeros_like(carry_ref)

    x = h_ref[...]
    u = _rmsnorm(x, g_ref[...]).astype(BF16)
    proj = _dot(u, w_ref[...])
    scale = HEAD_DIM ** -0.5 * LOG2E

    def tile(k):
        return proj[:, k * LANES:(k + 1) * LANES]

    def put(k, v):
        qkv_ref[:, k * LANES:(k + 1) * LANES] = v.astype(BF16)

    for k in ATT_NORM_Q_SWA:
        put(k, _head_rmsnorm(tile(k), gains_ref[0:1, :]) * scale)
    for k in ATT_NORM_K_SWA:
        put(k, _head_rmsnorm(tile(k), gains_ref[1:2, :]))
    for k in ATT_NORM_Q_FOX:
        put(k, _head_rmsnorm(tile(k), gains_ref[2:3, :]) * scale)
    for k in ATT_NORM_K_FOX:
        put(k, _head_rmsnorm(tile(k), gains_ref[3:4, :]))
    for k in ATT_COPY:
        put(k, tile(k))

    fl = _dot(u, wf_ref[...]) + fb_ref[...]
    log_f = -_softplus(-fl) * LOG2E
    tm = log_f.shape[0]
    cum = _cumsum_rows(log_f, _lower_tri(tm).astype(BF16)) + carry_ref[0:1, :]
    cum_ref[...] = cum
    carry_ref[0:1, :] = cum[tm - 1:tm, :]


def _attn_in(h, g, w, wf, fb, gains, *, tm):
    b, s, d = h.shape
    n = w.shape[1]
    return pl.pallas_call(
        _attn_in_body,
        out_shape=(jax.ShapeDtypeStruct((b, s, n), BF16),
                   jax.ShapeDtypeStruct((b, s, LANES), F32)),
        grid=(b, s // tm),
        in_specs=[
            pl.BlockSpec((None, tm, d), lambda bi, i: (bi, i, 0)),
            _resident((1, d)),
            _resident((d, n)),
            _resident((d, LANES)),
            _resident((1, LANES)),
            _resident((4, LANES)),
        ],
        out_specs=(pl.BlockSpec((None, tm, n), lambda bi, i: (bi, i, 0)),
                   pl.BlockSpec((None, tm, LANES), lambda bi, i: (bi, i, 0))),
        scratch_shapes=[pltpu.VMEM((8, LANES), F32)],
        compiler_params=_cparams(2, sequential=True),
        name="attn_in",
    )(h, g, w, wf, fb, gains)


def _swa_block(sinks_ref, q, kv, mask):
    blk = SWA_BLOCK
    nkeys = kv.shape[0]
    lo = _lane_lo((blk, LANES))
    ones = jnp.ones((nkeys, LANES), BF16)
    mask2 = jnp.concatenate([mask, mask], axis=0)
    first_head_rows = lax.broadcasted_iota(jnp.int32, (2 * blk, 1), 0) < blk
    outs = {}
    for kv_head in range(SWA_KV_HEADS):
        tiles = (2 * kv_head, 2 * kv_head + 1)
        for j in range(2):
            var = 0 if kv_head == j else 1
            kk = kv[:, var * LANES:(var + 1) * LANES]
            v_aug = jnp.concatenate([kv[:, (2 + var) * LANES:(3 + var) * LANES], ones], axis=1)
            sel = lo if j == 0 else ~lo
            qm = jnp.concatenate(
                [jnp.where(sel, q[:, t * LANES:(t + 1) * LANES], jnp.zeros((blk, LANES), BF16))
                 for t in tiles], axis=0)
            sink = jnp.where(first_head_rows, sinks_ref[2 * tiles[0] + j] * LOG2E,
                             sinks_ref[2 * tiles[1] + j] * LOG2E)
            sc = jnp.where(mask2, _dot_nt(qm, kk), -jnp.inf)
            m = jnp.maximum(jnp.max(sc, axis=-1, keepdims=True), sink)
            pv = _dot(jnp.exp2(sc - m).astype(BF16), v_aug)
            o = pv[:, :LANES] / (pv[:, LANES:] + jnp.exp2(sink - m))
            outs[tiles[0], j] = o[:blk, :]
            outs[tiles[1], j] = o[blk:, :]
    return jnp.concatenate([jnp.where(lo, outs[t, 0], outs[t, 1])
                            for t in range(SWA_Q_HEADS // 2)], axis=1).astype(BF16)


def _swa_body(sinks_ref, q_ref, kv_ref, o_ref):
    blk = SWA_BLOCK
    r = lax.broadcasted_iota(jnp.int32, (blk, 2 * blk), 0)
    c = lax.broadcasted_iota(jnp.int32, (blk, 2 * blk), 1)
    o_ref[0:blk, :] = _swa_block(sinks_ref, q_ref[0:blk, :], kv_ref[0:blk, :], _lower_tri(blk))
    window = (c > r) & (c <= r + blk)

    def step(n, carry):
        q0 = pl.multiple_of(n * blk, blk)
        k0 = pl.multiple_of((n - 1) * blk, blk)
        o_ref[pl.ds(q0, blk), :] = _swa_block(sinks_ref, q_ref[pl.ds(q0, blk), :],
                                              kv_ref[pl.ds(k0, 2 * blk), :], window)
        return carry

    lax.fori_loop(1, q_ref.shape[0] // blk, step, 0)


def _swa(sinks, qkv):
    b, s, _ = qkv.shape
    qw = SWA_Q_HEADS * HEAD_DIM
    return pl.pallas_call(
        _swa_body,
        out_shape=jax.ShapeDtypeStruct((b, s, qw), BF16),
        grid_spec=pltpu.PrefetchScalarGridSpec(
            num_scalar_prefetch=1,
            grid=(b,),
            in_specs=[
                pl.BlockSpec((None, s, qw), lambda bi, sk: (bi, 0, 0)),
                pl.BlockSpec((None, s, qw), lambda bi, sk: (bi, 0, 1)),
            ],
            out_specs=pl.BlockSpec((None, s, qw), lambda bi, sk: (bi, 0, 0)),
        ),
        compiler_params=_cparams(1),
        name="swa",
    )(sinks, qkv, qkv)


FOX_Q_TILE0 = 8
FOX_K_TILE0 = 12
FOX_V_TILE0 = 16


def _fox_body(q_ref, k_ref, v_ref, fq_ref, fk_ref, o_ref, m_ref, acc_ref, sa_ref, sb_ref, *, tq):
    hp = pl.program_id(1)
    i = pl.program_id(2)
    q = q_ref[...]
    fq_all = fq_ref[...]
    lo = _lane_lo((tq, LANES))
    lane = lax.broadcasted_iota(jnp.int32, (tq, LANES), 1)
    zero = jnp.zeros_like(q)
    q2 = jnp.concatenate([jnp.where(lo, q, zero), jnp.where(lo, zero, q)], axis=0)
    fq0 = jnp.sum(jnp.where(lane == 2 * hp, fq_all, 0.0), axis=-1, keepdims=True)
    fq1 = jnp.sum(jnp.where(lane == 2 * hp + 1, fq_all, 0.0), axis=-1, keepdims=True)
    ones = jnp.ones((tq, LANES), BF16)

    m_ref[...] = jnp.full(m_ref.shape, NEG_BIG, F32)
    acc_ref[...] = jnp.zeros(acc_ref.shape, F32)

    def scores(jb, masked):
        start = pl.multiple_of(jb * tq, tq)
        kb = k_ref[pl.ds(start, tq), :]
        fk = fk_ref[jb]
        sc = _dot_nt(q2, kb) + jnp.concatenate([fq0 - fk[0:1, :], fq1 - fk[1:2, :]], axis=0)
        if masked:
            tri = _lower_tri(tq)
            sc = jnp.where(jnp.concatenate([tri, tri], axis=0), sc, NEG_BIG)
        return sc

    def accumulate(sc, jb):
        start = pl.multiple_of(jb * tq, tq)
        v_aug = jnp.concatenate([v_ref[pl.ds(start, tq), :], ones], axis=1)
        m_old = m_ref[...]
        m_new = jnp.maximum(m_old, jnp.max(sc, axis=-1, keepdims=True))
        alpha = jnp.exp2(m_old - m_new)
        p = jnp.concatenate([jnp.exp2(sc[:, t * LANES:(t + 1) * LANES] - m_new)
                             for t in range(tq // LANES)], axis=1).astype(BF16)
        pv = _dot(p, v_aug)
        acc_ref[:, :LANES] = alpha * acc_ref[:, :LANES] + pv[:, :LANES]
        acc_ref[:, LANES:] = alpha * acc_ref[:, LANES:] + pv[:, LANES:]
        m_ref[...] = m_new

    sa_ref[...] = scores(i, True)

    def visited(n):
        return jnp.where(n == 0, i, n - 1)

    def pair(u, carry):
        sb_ref[...] = scores(2 * u, False)
        accumulate(sa_ref[...], visited(2 * u))
        sa_ref[...] = scores(2 * u + 1, False)
        accumulate(sb_ref[...], 2 * u)
        return carry

    lax.fori_loop(0, i // 2, pair, 0)

    @pl.when(i % 2 == 1)
    def _():
        sb_ref[...] = scores(i - 1, False)
        accumulate(sa_ref[...], visited(i - 1))
        accumulate(sb_ref[...], i - 1)

    @pl.when(i % 2 == 0)
    def _():
        accumulate(sa_ref[...], visited(i))

    acc = acc_ref[...]
    o = acc[:, :LANES] / acc[:, LANES:]
    o_ref[...] = jnp.where(lo, o[:tq, :], o[tq:, :]).astype(BF16)


def _fox(qkv, cum_col, cum_row, *, tq):
    b, s, _ = qkv.shape
    nq = s // tq
    npairs = FOX_HEADS // 2
    return pl.pallas_call(
        functools.partial(_fox_body, tq=tq),
        out_shape=jax.ShapeDtypeStruct((b, s, FOX_HEADS * HEAD_DIM), BF16),
        grid=(b, npairs, nq),
        in_specs=[
            pl.BlockSpec((None, tq, LANES), lambda bi, hp, i: (bi, i, FOX_Q_TILE0 + hp)),
            pl.BlockSpec((None, s, LANES), lambda bi, hp, i: (bi, 0, FOX_K_TILE0 + hp)),
            pl.BlockSpec((None, s, LANES), lambda bi, hp, i: (bi, 0, FOX_V_TILE0 + hp)),
            pl.BlockSpec((None, tq, LANES), lambda bi, hp, i: (bi, i, 0)),
            pl.BlockSpec((None, None, nq, 2, tq), lambda bi, hp, i: (bi, hp, 0, 0, 0)),
        ],
        out_specs=pl.BlockSpec((None, tq, LANES), lambda bi, hp, i: (bi, i, hp)),
        scratch_shapes=[pltpu.VMEM((2 * tq, LANES), F32), pltpu.VMEM((2 * tq, 2 * LANES), F32),
                        pltpu.VMEM((2 * tq, tq), F32), pltpu.VMEM((2 * tq, tq), F32)],
        compiler_params=_cparams(3),
        name="fox",
    )(qkv, qkv, qkv, cum_col, cum_row)


SSM_PROJ_BLOCK = 512


def _ssm_project_pieces(h_ref, g_ref, wz_ref, wx_ref, wdt_ref, z_buf, x_buf, dt_buf):
    u = _rmsnorm(h_ref[...], g_ref[...]).astype(BF16)

    def piece(buf, w_ref, c0, c1):
        def run():
            buf[:, c0:c1] = _dot(u, w_ref[:, c0:c1])
        return run

    pieces = [piece(dt_buf, wdt_ref, 0, LANES)]
    for buf, w_ref in ((x_buf, wx_ref), (z_buf, wz_ref)):
        n = w_ref.shape[1]
        pieces += [piece(buf, w_ref, c0, min(c0 + SSM_PROJ_BLOCK, n))
                   for c0 in range(0, n, SSM_PROJ_BLOCK)]
    return pieces


def _ssm_mix(hprev_ref, z_buf, x_buf, dt_buf, cw_ref, cb_ref, dtb_ref, a_ref, dsk_ref, ng_ref,
             wout_ref, o_ref, state_ref, tail_ref, side_work, *, tm):
    d_inner = SSM_HEADS * HEAD_DIM
    bc_w = SSM_GROUPS * SSM_STATE
    L = SSM_CHUNK
    def do_side_work(n):
        for _ in range(n):
            if side_work:
                side_work.pop(0)()

    conv_w = x_buf.shape[1]
    row8 = lax.broadcasted_iota(jnp.int32, (8, SSM_PROJ_BLOCK), 0)
    xbc_blocks = []
    for c0 in range(0, conv_w, SSM_PROJ_BLOCK):
        cols = slice(c0, c0 + SSM_PROJ_BLOCK)
        cur = x_buf[:, cols]
        prev = tail_ref[:, cols]
        conv = cur * cw_ref[SSM_CONV - 1:SSM_CONV, cols] + cb_ref[:, cols]
        for sh in range(1, SSM_CONV):
            cur_s = pltpu.roll(cur, sh, axis=0)
            prev_s = pltpu.roll(prev, sh, axis=0)
            top = jnp.where(row8 < sh, prev_s, cur_s[0:8, :])
            shifted = jnp.concatenate([top, cur_s[8:, :]], axis=0)
            conv = conv + shifted * cw_ref[SSM_CONV - 1 - sh:SSM_CONV - sh, cols]
        tail_ref[:, cols] = cur[tm - 8:tm, :]
        xbc_blocks.append(conv * _sigmoid(conv))
        do_side_work(1)
    xbc = jnp.concatenate(xbc_blocks, axis=1)

    a2 = -jnp.exp(a_ref[...]) * LOG2E
    dt_all = _softplus(dt_buf[...] + dtb_ref[...])
    tri = _lower_tri(L)
    tri_bf = tri.astype(BF16)
    lo = _lane_lo((L, LANES))
    lo_row = _lane_lo((1, LANES))
    heads_per_group = SSM_HEADS // SSM_GROUPS

    for c in range(tm // L):
        rows = slice(c * L, (c + 1) * L)
        dt = dt_all[rows, :]
        a_cs = _cumsum_rows(dt * a2, tri_bf)
        a_cs_t = a_cs.T
        r_t = a_cs_t - jnp.log2(dt.T)
        y_tiles = []
        for g in range(SSM_GROUPS):
            b_g = xbc[rows, d_inner + g * SSM_STATE:d_inner + (g + 1) * SSM_STATE]
            c_g = xbc[rows, d_inner + bc_w + g * SSM_STATE:d_inner + bc_w + (g + 1) * SSM_STATE]
            cb = _dot_nt(c_g.astype(BF16), b_g.astype(BF16))
            b_t = b_g.T
            for pr in range(heads_per_group // 2):
                hp = g * (heads_per_group // 2) + pr
                x_pair = xbc[rows, hp * LANES:(hp + 1) * LANES]
                xb = x_pair.astype(BF16)
                st = state_ref[hp]
                lhs_y, lhs_s, cds = [], [], []
                for j in range(2):
                    h = 2 * hp + j
                    a_col = jnp.broadcast_to(a_cs[:, h:h + 1], (L, L))
                    r_row = r_t[h:h + 1, :]
                    a_last = a_cs_t[h:h + 1, L - 1:L]
                    m_diag = cb * jnp.exp2(jnp.where(tri, a_col - r_row, -jnp.inf))
                    c_off = c_g * jnp.exp2(a_col)
                    lhs_y.append(jnp.concatenate([m_diag, c_off], axis=1).astype(BF16))
                    lhs_s.append((b_t * jnp.exp2(a_last - r_row)).astype(BF16))
                    cds.append(jnp.exp2(a_last))
                rhs = jnp.concatenate([xb, st.astype(BF16)], axis=0)
                y2 = _dot(jnp.concatenate(lhs_y, axis=0), rhs)
                y_tiles.append(jnp.where(lo, y2[0:L, :], y2[L:2 * L, :])
                               + dsk_ref[:, hp * LANES:(hp + 1) * LANES] * x_pair)
                s2 = _dot(jnp.concatenate(lhs_s, axis=0), xb)
                cd = jnp.where(lo_row, cds[0], cds[1])
                state_ref[hp] = st * cd + jnp.where(lo, s2[0:SSM_STATE, :], s2[SSM_STATE:, :])
            do_side_work(1)
        y = jnp.concatenate(y_tiles, axis=1)
        zc = z_buf[rows, :]
        y = y * (zc * _sigmoid(zc))
        gw = d_inner // SSM_GROUPS
        normed = []
        for g in range(SSM_GROUPS):
            yg = y[:, g * gw:(g + 1) * gw]
            ms = jnp.mean(yg * yg, axis=-1, keepdims=True)
            normed.append(yg * lax.rsqrt(ms + RMS_EPS))
        y_out = (jnp.concatenate(normed, axis=1) * ng_ref[...]).astype(BF16)
        o_ref[rows, :] = hprev_ref[rows, :] + _dot(y_out, wout_ref[...])

    do_side_work(len(side_work))


def _ssm_body(h_ref, hprev_ref, g_ref, wz_ref, wx_ref, wdt_ref, cw_ref, cb_ref, dtb_ref, a_ref,
              dsk_ref, ng_ref, wout_ref, o_ref, z0, x0, d0, z1, x1, d1, state_ref, tail_ref,
              *, tm, tiles_per_seq):
    t = pl.program_id(0)

    @pl.when(t == 0)
    def _():
        z1[...] = jnp.zeros_like(z1)
        x1[...] = jnp.zeros_like(x1)
        d1[...] = jnp.zeros_like(d1)

    @pl.when((t == 0) | ((t - 1) % tiles_per_seq == 0))
    def _():
        state_ref[...] = jnp.zeros_like(state_ref)
        tail_ref[...] = jnp.zeros_like(tail_ref)

    def run(pbuf, mbuf):
        pieces = _ssm_project_pieces(h_ref, g_ref, wz_ref, wx_ref, wdt_ref, *pbuf)
        _ssm_mix(hprev_ref, *mbuf, cw_ref, cb_ref, dtb_ref, a_ref, dsk_ref, ng_ref, wout_ref,
                 o_ref, state_ref, tail_ref, pieces, tm=tm)

    @pl.when(t % 2 == 0)
    def _():
        run((z0, x0, d0), (z1, x1, d1))

    @pl.when(t % 2 == 1)
    def _():
        run((z1, x1, d1), (z0, x0, d0))


def _ssm_fused(h, g, wz, wx, wdt, cw, cb, dtb, a_log, dsk, ng, wout, *, tm, seq):
    t, d = h.shape
    d_inner, conv_w = wz.shape[1], wx.shape[1]
    n_tiles = t // tm
    return pl.pallas_call(
        functools.partial(_ssm_body, tm=tm, tiles_per_seq=seq // tm),
        out_shape=jax.ShapeDtypeStruct((t, d), F32),
        grid=(n_tiles + 1,),
        in_specs=[
            pl.BlockSpec((tm, d), lambda i: (jnp.minimum(i, n_tiles - 1), 0)),
            pl.BlockSpec((tm, d), lambda i: (jnp.maximum(i - 1, 0), 0)),
            _resident((1, d)),
            _resident((d, d_inner)),
            _resident((d, conv_w)),
            _resident((d, LANES)),
            _resident((SSM_CONV, conv_w)),
            _resident((1, conv_w)),
            _resident((1, LANES)),
            _resident((1, LANES)),
            _resident((1, d_inner)),
            _resident((1, d_inner)),
            _resident((d_inner, d)),
        ],
        out_specs=pl.BlockSpec((tm, d), lambda i: (jnp.maximum(i - 1, 0), 0)),
        scratch_shapes=[
            pltpu.VMEM((tm, d_inner), F32), pltpu.VMEM((tm, conv_w), F32), pltpu.VMEM((tm, LANES), F32),
            pltpu.VMEM((tm, d_inner), F32), pltpu.VMEM((tm, conv_w), F32), pltpu.VMEM((tm, LANES), F32),
            pltpu.VMEM((SSM_HEADS // 2, SSM_STATE, LANES), F32),
            pltpu.VMEM((8, conv_w), F32),
        ],
        compiler_params=_cparams(1, sequential=True, vmem_limit=VMEM_LIMIT_SSM),
        name="ssm",
    )(h, h, g, wz, wx, wdt, cw, cb, dtb, a_log, dsk, ng, wout)


def _pad_lanes(v):
    v = v.reshape(1, -1).astype(F32)
    return jnp.pad(v, ((0, 0), (0, LANES - v.shape[1])))


def _swap_heads(w):
    return jnp.concatenate([w[:, HEAD_DIM:], w[:, :HEAD_DIM]], axis=1)


def _attention_layer(h3, g, w_in, forget_bias, swa_qn, swa_kn, sinks, fox_qn, fox_kn, w_out,
                     *, tm_in, tq):
    b, s, d = h3.shape
    qa_w = SWA_Q_HEADS * HEAD_DIM
    kv_w = SWA_KV_HEADS * HEAD_DIM
    fox_w = FOX_HEADS * HEAD_DIM
    o = 0
    w_qa = w_in[:, o:o + qa_w]; o += qa_w
    w_ka = w_in[:, o:o + kv_w]; o += kv_w
    w_va = w_in[:, o:o + kv_w]; o += kv_w
    w_fox = w_in[:, o:o + 3 * fox_w]; o += 3 * fox_w
    w_f = w_in[:, o:]
    w_main = jnp.concatenate(
        [w_qa, w_ka, _swap_heads(w_ka), w_va, _swap_heads(w_va), w_fox], axis=1).astype(BF16)
    w_fp = jnp.pad(w_f, ((0, 0), (0, LANES - w_f.shape[1]))).astype(BF16)
    gains = jnp.stack([jnp.tile(v.astype(F32), 2) for v in (swa_qn, swa_kn, fox_qn, fox_kn)])
    qkv, cum = _attn_in(h3, g.reshape(1, d), w_main, w_fp, _pad_lanes(forget_bias), gains, tm=tm_in)
    o_a = _swa(sinks.astype(F32), qkv)
    nq = s // tq
    cum_row = cum[:, :, :FOX_HEADS].reshape(b, nq, tq, FOX_HEADS // 2, 2).transpose(0, 3, 1, 4, 2)
    o_b = _fox(qkv, cum, cum_row, tq=tq)
    w_out_b = w_out.astype(BF16)
    return (o_a.reshape(b * s, qa_w), o_b.reshape(b * s, fox_w), w_out_b[:qa_w], w_out_b[qa_w:])


def _ssm_layer(h3, g, w_in, conv_w, conv_b, dt_bias, a_log, d_skip, norm_g, w_out,
               *, tm):
    b, s, d = h3.shape
    d_inner = SSM_HEADS * HEAD_DIM
    cw = d_inner + 2 * SSM_GROUPS * SSM_STATE
    w_z = w_in[:, :d_inner].astype(BF16)
    w_x = w_in[:, d_inner:d_inner + cw].astype(BF16)
    w_dt = w_in[:, d_inner + cw:]
    w_dt = jnp.pad(w_dt, ((0, 0), (0, LANES - w_dt.shape[1]))).astype(BF16)
    dsk = jnp.repeat(d_skip.astype(F32), HEAD_DIM).reshape(1, d_inner)
    out = _ssm_fused(h3.reshape(b * s, d), g.reshape(1, d), w_z, w_x, w_dt,
                     conv_w.astype(F32), conv_b.reshape(1, cw).astype(F32), _pad_lanes(dt_bias),
                     _pad_lanes(a_log), dsk, norm_g.reshape(1, d_inner).astype(F32),
                     w_out.astype(BF16), tm=tm, seq=s)
    return out.reshape(b, s, d)


def kernel(x, ffn1_norm, ffn1_w_gate, ffn1_w_up, ffn1_w_down, mix_norm, ffn2_norm, ffn2_w_gate, ffn2_w_up, ffn2_w_down, attn_w_in, attn_forget_bias, swa_q_norm, swa_k_norm, swa_sinks, fox_q_norm, fox_k_norm, attn_w_out, ssm_w_in, ssm_conv_w, ssm_conv_b, ssm_dt_bias, ssm_a_log, ssm_d_skip, ssm_norm, ssm_w_out):
    b, s, d = x.shape
    depth = ffn1_norm.shape[0]
    t = b * s
    tm = min(512, t)
    tq = min(512, s)

    ffn1 = (ffn1_norm.reshape(depth, 1, d), ffn1_w_gate.astype(BF16), ffn1_w_up.astype(BF16),
            ffn1_w_down.astype(BF16))
    ffn2 = (ffn2_norm.reshape(depth, 1, d), ffn2_w_gate.astype(BF16), ffn2_w_up.astype(BF16),
            ffn2_w_down.astype(BF16))

    def ffn(h3, params, layer, mixer_out=None):
        return _ffn(h3.reshape(t, d), *params, layer=layer, tm=tm,
                    mixer_out=mixer_out).reshape(b, s, d)

    h = x
    for layer in range(depth):
        h = ffn(h, ffn1, layer)
        i = layer // 2
        mixer_out = None
        if layer % 2 == 0:
            mixer_out = _attention_layer(h, mix_norm[layer], attn_w_in[i], attn_forget_bias[i],
                                         swa_q_norm[i], swa_k_norm[i], swa_sinks[i],
                                         fox_q_norm[i], fox_k_norm[i], attn_w_out[i],
                                         tm_in=min(512, s), tq=tq)
        else:
            h = _ssm_layer(h, mix_norm[layer], ssm_w_in[i], ssm_conv_w[i], ssm_conv_b[i],
                           ssm_dt_bias[i], ssm_a_log[i], ssm_d_skip[i], ssm_norm[i],
                           ssm_w_out[i], tm=min(512, s))
        h = ffn(h, ffn2, layer, mixer_out)
    return h
```

```python
import functools
import math

import jax
import jax.numpy as jnp
from jax import lax
from jax.experimental import pallas as pl
from jax.experimental.pallas import tpu as pltpu

F32 = jnp.float32
BF16 = jnp.bfloat16

RMS_EPS = 1e-6
HEAD_DIM = 64
LANES = 128
SWA_Q_HEADS = 8
SWA_KV_HEADS = 2
SWA_BLOCK = 128
FOX_HEADS = 8
SSM_HEADS = 32
SSM_GROUPS = 4
SSM_STATE = 128
SSM_CHUNK = 128
SSM_CONV = 4
NEG_BIG = -1e30
LOG2E = math.log2(math.e)

VMEM_LIMIT = 56 * 1024 * 1024
VMEM_LIMIT_SSM = 60 * 1024 * 1024


def _cparams(n_axes, sequential=False, vmem_limit=VMEM_LIMIT):
    sem = ("arbitrary",) * n_axes if sequential else ("parallel",) * n_axes
    return pltpu.CompilerParams(dimension_semantics=sem, vmem_limit_bytes=vmem_limit)


def _resident(shape):
    nd = len(shape)
    return pl.BlockSpec(shape, lambda *_: (0,) * nd, pipeline_mode=pl.Buffered(1))


def _rmsnorm(x, g):
    ms = jnp.mean(x * x, axis=-1, keepdims=True)
    return x * lax.rsqrt(ms + RMS_EPS) * g


def _sigmoid(x):
    return 1.0 / (1.0 + jnp.exp(-x))


def _softplus(x):
    return jnp.maximum(x, 0.0) + jnp.log1p(jnp.exp(-jnp.abs(x)))


def _lane_lo(shape):
    return lax.broadcasted_iota(jnp.int32, shape, len(shape) - 1) < HEAD_DIM


def _head_rmsnorm(x, g):
    lo = _lane_lo(x.shape)
    sq = x * x
    s_lo = jnp.sum(jnp.where(lo, sq, 0.0), axis=-1, keepdims=True)
    s_hi = jnp.sum(jnp.where(lo, 0.0, sq), axis=-1, keepdims=True)
    ms = jnp.where(lo, s_lo, s_hi) * (1.0 / HEAD_DIM)
    return x * lax.rsqrt(ms + RMS_EPS) * g


def _dot(a, b):
    return jnp.dot(a, b, preferred_element_type=F32)


def _dot_nt(a, b):
    return lax.dot_general(a, b, (((1,), (1,)), ((), ())), preferred_element_type=F32)


def _lower_tri(n):
    r = lax.broadcasted_iota(jnp.int32, (n, n), 0)
    c = lax.broadcasted_iota(jnp.int32, (n, n), 1)
    return r >= c


def _cumsum_rows(x, tri_bf):
    hi = x.astype(BF16)
    r1 = x - hi.astype(F32)
    mid = r1.astype(BF16)
    lo = (r1 - mid.astype(F32)).astype(BF16)
    return _dot(tri_bf, hi) + _dot(tri_bf, mid) + _dot(tri_bf, lo)


def _ffn_body(*refs, with_mixer_out):
    if with_mixer_out:
        h_ref, a_ref, b_ref, wa_ref, wb_ref, g_ref, wg_ref, wu_ref, wd_ref, o_ref = refs
        x = h_ref[...] + _dot(a_ref[...], wa_ref[...]) + _dot(b_ref[...], wb_ref[...])
    else:
        h_ref, g_ref, wg_ref, wu_ref, wd_ref, o_ref = refs
        x = h_ref[...]
    xn = _rmsnorm(x, g_ref[...]).astype(BF16)
    gate = _dot(xn, wg_ref[...])
    up = _dot(xn, wu_ref[...])
    act = (gate * _sigmoid(gate) * up).astype(BF16)
    o_ref[...] = x + 0.5 * _dot(act, wd_ref[...])


def _layer_resident(shape, layer):
    nd = len(shape)
    return pl.BlockSpec((None,) + tuple(shape), lambda *_: (layer,) + (0,) * nd,
                        pipeline_mode=pl.Buffered(1))


def _ffn(h, g, wg, wu, wd, *, layer, tm, mixer_out=None):
    t, d = h.shape
    dff = wg.shape[2]
    tile = lambda width: pl.BlockSpec((tm, width), lambda i: (i, 0))
    mix_args, mix_specs = (), []
    if mixer_out is not None:
        o_a, o_b, w_a, w_b = mix_args = mixer_out
        mix_specs = [tile(o_a.shape[1]), tile(o_b.shape[1]), _resident(w_a.shape), _resident(w_b.shape)]
    return pl.pallas_call(
        functools.partial(_ffn_body, with_mixer_out=mixer_out is not None),
        out_shape=jax.ShapeDtypeStruct((t, d), F32),
        grid=(t // tm,),
        in_specs=[tile(d)] + mix_specs + [
            _layer_resident((1, d), layer),
            _layer_resident((d, dff), layer),
            _layer_resident((d, dff), layer),
            _layer_resident((dff, d), layer),
        ],
        out_specs=tile(d),
        compiler_params=_cparams(1),
        name="ffn",
    )(h, *mix_args, g, wg, wu, wd)


ATT_NORM_Q_SWA = (0, 1, 2, 3)
ATT_NORM_K_SWA = (4, 5)
ATT_COPY = (6, 7, 16, 17, 18, 19)
ATT_NORM_Q_FOX = (8, 9, 10, 11)
ATT_NORM_K_FOX = (12, 13, 14, 15)
ATT_TILES = 20


def _attn_in_body(h_ref, g_ref, w_ref, wf_ref, fb_ref, gains_ref, qkv_ref, cum_ref, carry_ref):
    i = pl.program_id(1)

    @pl.when(i == 0)
    def _():
        carry_ref[...] = jnp.zeros_like(carry_ref)

    x = h_ref[...]
    u = _rmsnorm(x, g_ref[...]).astype(BF16)
    proj = _dot(u, w_ref[...])
    scale = HEAD_DIM ** -0.5 * LOG2E

    def tile(k):
        return proj[:, k * LANES:(k + 1) * LANES]

    def put(k, v):
        qkv_ref[:, k * LANES:(k + 1) * LANES] = v.astype(BF16)

    for k in ATT_NORM_Q_SWA:
        put(k, _head_rmsnorm(tile(k), gains_ref[0:1, :]) * scale)
    for k in ATT_NORM_K_SWA:
        put(k, _head_rmsnorm(tile(k), gains_ref[1:2, :]))
    for k in ATT_NORM_Q_FOX:
        put(k, _head_rmsnorm(tile(k), gains_ref[2:3, :]) * scale)
    for k in ATT_NORM_K_FOX:
        put(k, _head_rmsnorm(tile(k), gains_ref[3:4, :]))
    for k in ATT_COPY:
        put(k, tile(k))

    fl = _dot(u, wf_ref[...]) + fb_ref[...]
    log_f = -_softplus(-fl) * LOG2E
    tm = log_f.shape[0]
    cum = _cumsum_rows(log_f, _lower_tri(tm).astype(BF16)) + carry_ref[0:1, :]
    cum_ref[...] = cum
    carry_ref[0:1, :] = cum[tm - 1:tm, :]


def _attn_in(h, g, w, wf, fb, gains, *, tm):
    b, s, d = h.shape
    n = w.shape[1]
    return pl.pallas_call(
        _attn_in_body,
        out_shape=(jax.ShapeDtypeStruct((b, s, n), BF16),
                   jax.ShapeDtypeStruct((b, s, LANES), F32)),
        grid=(b, s // tm),
        in_specs=[
            pl.BlockSpec((None, tm, d), lambda bi, i: (bi, i, 0)),
            _resident((1, d)),
            _resident((d, n)),
            _resident((d, LANES)),
            _resident((1, LANES)),
            _resident((4, LANES)),
        ],
        out_specs=(pl.BlockSpec((None, tm, n), lambda bi, i: (bi, i, 0)),
                   pl.BlockSpec((None, tm, LANES), lambda bi, i: (bi, i, 0))),
        scratch_shapes=[pltpu.VMEM((8, LANES), F32)],
        compiler_params=_cparams(2, sequential=True),
        name="attn_in",
    )(h, g, w, wf, fb, gains)


def _swa_block(sinks_ref, q, kv, mask):
    blk = SWA_BLOCK
    nkeys = kv.shape[0]
    lo = _lane_lo((blk, LANES))
    ones = jnp.ones((nkeys, LANES), BF16)
    mask2 = jnp.concatenate([mask, mask], axis=0)
    first_head_rows = lax.broadcasted_iota(jnp.int32, (2 * blk, 1), 0) < blk
    outs = {}
    for kv_head in range(SWA_KV_HEADS):
        tiles = (2 * kv_head, 2 * kv_head + 1)
        for j in range(2):
            var = 0 if kv_head == j else 1
            kk = kv[:, var * LANES:(var + 1) * LANES]
            v_aug = jnp.concatenate([kv[:, (2 + var) * LANES:(3 + var) * LANES], ones], axis=1)
            sel = lo if j == 0 else ~lo
            qm = jnp.concatenate(
                [jnp.where(sel, q[:, t * LANES:(t + 1) * LANES], jnp.zeros((blk, LANES), BF16))
                 for t in tiles], axis=0)
            sink = jnp.where(first_head_rows, sinks_ref[2 * tiles[0] + j] * LOG2E,
                             sinks_ref[2 * tiles[1] + j] * LOG2E)
            sc = jnp.where(mask2, _dot_nt(qm, kk), -jnp.inf)
            m = jnp.maximum(jnp.max(sc, axis=-1, keepdims=True), sink)
            pv = _dot(jnp.exp2(sc - m).astype(BF16), v_aug)
            o = pv[:, :LANES] / (pv[:, LANES:] + jnp.exp2(sink - m))
            outs[tiles[0], j] = o[:blk, :]
            outs[tiles[1], j] = o[blk:, :]
    return jnp.concatenate([jnp.where(lo, outs[t, 0], outs[t, 1])
                            for t in range(SWA_Q_HEADS // 2)], axis=1).astype(BF16)


def _swa_body(sinks_ref, q_ref, kv_ref, o_ref):
    blk = SWA_BLOCK
    r = lax.broadcasted_iota(jnp.int32, (blk, 2 * blk), 0)
    c = lax.broadcasted_iota(jnp.int32, (blk, 2 * blk), 1)
    o_ref[0:blk, :] = _swa_block(sinks_ref, q_ref[0:blk, :], kv_ref[0:blk, :], _lower_tri(blk))
    window = (c > r) & (c <= r + blk)

    def step(n, carry):
        q0 = pl.multiple_of(n * blk, blk)
        k0 = pl.multiple_of((n - 1) * blk, blk)
        o_ref[pl.ds(q0, blk), :] = _swa_block(sinks_ref, q_ref[pl.ds(q0, blk), :],
                                              kv_ref[pl.ds(k0, 2 * blk), :], window)
        return carry

    lax.fori_loop(1, q_ref.shape[0] // blk, step, 0)


def _swa(sinks, qkv):
    b, s, _ = qkv.shape
    qw = SWA_Q_HEADS * HEAD_DIM
    return pl.pallas_call(
        _swa_body,
        out_shape=jax.ShapeDtypeStruct((b, s, qw), BF16),
        grid_spec=pltpu.PrefetchScalarGridSpec(
            num_scalar_prefetch=1,
            grid=(b,),
            in_specs=[
                pl.BlockSpec((None, s, qw), lambda bi, sk: (bi, 0, 0)),
                pl.BlockSpec((None, s, qw), lambda bi, sk: (bi, 0, 1)),
            ],
            out_specs=pl.BlockSpec((None, s, qw), lambda bi, sk: (bi, 0, 0)),
        ),
        compiler_params=_cparams(1),
        name="swa",
    )(sinks, qkv, qkv)


FOX_Q_TILE0 = 8
FOX_K_TILE0 = 12
FOX_V_TILE0 = 16


def _fox_body(q_ref, k_ref, v_ref, fq_ref, fk_ref, o_ref, m_ref, acc_ref, sa_ref, sb_ref, *, tq):
    hp = pl.program_id(1)
    i = pl.program_id(2)
    q = q_ref[...]
    fq_all = fq_ref[...]
    lo = _lane_lo((tq, LANES))
    lane = lax.broadcasted_iota(jnp.int32, (tq, LANES), 1)
    zero = jnp.zeros_like(q)
    q2 = jnp.concatenate([jnp.where(lo, q, zero), jnp.where(lo, zero, q)], axis=0)
    fq0 = jnp.sum(jnp.where(lane == 2 * hp, fq_all, 0.0), axis=-1, keepdims=True)
    fq1 = jnp.sum(jnp.where(lane == 2 * hp + 1, fq_all, 0.0), axis=-1, keepdims=True)
    ones = jnp.ones((tq, LANES), BF16)

    m_ref[...] = jnp.full(m_ref.shape, NEG_BIG, F32)
    acc_ref[...] = jnp.zeros(acc_ref.shape, F32)

    def scores(jb, masked):
        start = pl.multiple_of(jb * tq, tq)
        kb = k_ref[pl.ds(start, tq), :]
        fk = fk_ref[jb]
        sc = _dot_nt(q2, kb) + jnp.concatenate([fq0 - fk[0:1, :], fq1 - fk[1:2, :]], axis=0)
        if masked:
            tri = _lower_tri(tq)
            sc = jnp.where(jnp.concatenate([tri, tri], axis=0), sc, NEG_BIG)
        return sc

    def accumulate(sc, jb):
        start = pl.multiple_of(jb * tq, tq)
        v_aug = jnp.concatenate([v_ref[pl.ds(start, tq), :], ones], axis=1)
        m_old = m_ref[...]
        m_new = jnp.maximum(m_old, jnp.max(sc, axis=-1, keepdims=True))
        alpha = jnp.exp2(m_old - m_new)
        p = jnp.concatenate([jnp.exp2(sc[:, t * LANES:(t + 1) * LANES] - m_new)
                             for t in range(tq // LANES)], axis=1).astype(BF16)
        pv = _dot(p, v_aug)
        acc_ref[:, :LANES] = alpha * acc_ref[:, :LANES] + pv[:, :LANES]
        acc_ref[:, LANES:] = alpha * acc_ref[:, LANES:] + pv[:, LANES:]
        m_ref[...] = m_new

    sa_ref[...] = scores(i, True)

    def visited(n):
        return jnp.where(n == 0, i, n - 1)

    def pair(u, carry):
        sb_ref[...] = scores(2 * u, False)
        accumulate(sa_ref[...], visited(2 * u))
        sa_ref[...] = scores(2 * u + 1, False)
        accumulate(sb_ref[...], 2 * u)
        return carry

    lax.fori_loop(0, i // 2, pair, 0)

    @pl.when(i % 2 == 1)
    def _():
        sb_ref[...] = scores(i - 1, False)
        accumulate(sa_ref[...], visited(i - 1))
        accumulate(sb_ref[...], i - 1)

    @pl.when(i % 2 == 0)
    def _():
        accumulate(sa_ref[...], visited(i))

    acc = acc_ref[...]
    o = acc[:, :LANES] / acc[:, LANES:]
    o_ref[...] = jnp.where(lo, o[:tq, :], o[tq:, :]).astype(BF16)


def _fox(qkv, cum_col, cum_row, *, tq):
    b, s, _ = qkv.shape
    nq = s // tq
    npairs = FOX_HEADS // 2
    return pl.pallas_call(
        functools.partial(_fox_body, tq=tq),
        out_shape=jax.ShapeDtypeStruct((b, s, FOX_HEADS * HEAD_DIM), BF16),
        grid=(b, npairs, nq),
        in_specs=[
            pl.BlockSpec((None, tq, LANES), lambda bi, hp, i: (bi, i, FOX_Q_TILE0 + hp)),
            pl.BlockSpec((None, s, LANES), lambda bi, hp, i: (bi, 0, FOX_K_TILE0 + hp)),
            pl.BlockSpec((None, s, LANES), lambda bi, hp, i: (bi, 0, FOX_V_TILE0 + hp)),
            pl.BlockSpec((None, tq, LANES), lambda bi, hp, i: (bi, i, 0)),
            pl.BlockSpec((None, None, nq, 2, tq), lambda bi, hp, i: (bi, hp, 0, 0, 0)),
        ],
        out_specs=pl.BlockSpec((None, tq, LANES), lambda bi, hp, i: (bi, i, hp)),
        scratch_shapes=[pltpu.VMEM((2 * tq, LANES), F32), pltpu.VMEM((2 * tq, 2 * LANES), F32),
                        pltpu.VMEM((2 * tq, tq), F32), pltpu.VMEM((2 * tq, tq), F32)],
        compiler_params=_cparams(3),
        name="fox",
    )(qkv, qkv, qkv, cum_col, cum_row)


SSM_CONV_BLOCK = 512
SSM_PROJ_BLOCK = 256


def _ssm_project_pieces(h_ref, g_ref, wz_ref, wx_ref, wdt_ref, z_buf, x_buf, dt_buf):
    u = _rmsnorm(h_ref[...], g_ref[...]).astype(BF16)

    def piece(buf, w_ref, c0, c1):
        def run():
            buf[:, c0:c1] = _dot(u, w_ref[:, c0:c1])
        return run

    pieces = [piece(dt_buf, wdt_ref, 0, LANES)]
    for buf, w_ref in ((x_buf, wx_ref), (z_buf, wz_ref)):
        n = w_ref.shape[1]
        pieces += [piece(buf, w_ref, c0, min(c0 + SSM_PROJ_BLOCK, n))
                   for c0 in range(0, n, SSM_PROJ_BLOCK)]
    return pieces


def _ssm_mix(hprev_ref, z_buf, x_buf, dt_buf, cw_ref, cb_ref, dtb_ref, a_ref, dsk_ref, ng_ref,
             wout_ref, o_ref, state_ref, tail_ref, side_work, *, tm):
    d_inner = SSM_HEADS * HEAD_DIM
    bc_w = SSM_GROUPS * SSM_STATE
    L = SSM_CHUNK
    stages = (tm // L) * SSM_GROUPS
    per_stage = -(-len(side_work) // stages)

    def do_side_work(n):
        for _ in range(n):
            if side_work:
                side_work.pop(0)()

    conv_w = x_buf.shape[1]
    row8 = lax.broadcasted_iota(jnp.int32, (8, SSM_CONV_BLOCK), 0)
    xbc_blocks = []
    for c0 in range(0, conv_w, SSM_CONV_BLOCK):
        cols = slice(c0, c0 + SSM_CONV_BLOCK)
        cur = x_buf[:, cols]
        prev = tail_ref[:, cols]
        conv = cur * cw_ref[SSM_CONV - 1:SSM_CONV, cols] + cb_ref[:, cols]
        for sh in range(1, SSM_CONV):
            cur_s = pltpu.roll(cur, sh, axis=0)
            prev_s = pltpu.roll(prev, sh, axis=0)
            top = jnp.where(row8 < sh, prev_s, cur_s[0:8, :])
            shifted = jnp.concatenate([top, cur_s[8:, :]], axis=0)
            conv = conv + shifted * cw_ref[SSM_CONV - 1 - sh:SSM_CONV - sh, cols]
        tail_ref[:, cols] = cur[tm - 8:tm, :]
        xbc_blocks.append(conv * _sigmoid(conv))
    xbc =jnp.concatenate(xbc_blocks, axis=1)

    a2 = -jnp.exp(a_ref[...]) * LOG2E
    dt_all = _softplus(dt_buf[...] + dtb_ref[...])
    tri = _lower_tri(L)
    tri_bf = tri.astype(BF16)
    lo = _lane_lo((L, LANES))
    lo_row = _lane_lo((1, LANES))
    heads_per_group = SSM_HEADS // SSM_GROUPS

    for c in range(tm // L):
        rows = slice(c * L, (c + 1) * L)
        dt = dt_all[rows, :]
        a_cs = _cumsum_rows(dt * a2, tri_bf)
        a_cs_t = a_cs.T
        r_t = a_cs_t - jnp.log2(dt.T)
        y_tiles = []
        for g in range(SSM_GROUPS):
            b_g = xbc[rows, d_inner + g * SSM_STATE:d_inner + (g + 1) * SSM_STATE]
            c_g = xbc[rows, d_inner + bc_w + g * SSM_STATE:d_inner + bc_w + (g + 1) * SSM_STATE]
            cb = _dot_nt(c_g.astype(BF16), b_g.astype(BF16))
            b_t = b_g.T
            for pr in range(heads_per_group // 2):
                hp = g * (heads_per_group // 2) + pr
                x_pair = xbc[rows, hp * LANES:(hp + 1) * LANES]
                xb = x_pair.astype(BF16)
                st = state_ref[hp]
                lhs_y, lhs_s, cds = [], [], []
                for j in range(2):
                    h = 2 * hp + j
                    a_col = jnp.broadcast_to(a_cs[:, h:h + 1], (L, L))
                    r_row = r_t[h:h + 1, :]
                    a_last = a_cs_t[h:h + 1, L - 1:L]
                    m_diag = cb * jnp.exp2(jnp.where(tri, a_col - r_row, -jnp.inf))
                    c_off = c_g * jnp.exp2(a_col)
                    lhs_y.append(jnp.concatenate([m_diag, c_off], axis=1).astype(BF16))
                    lhs_s.append((b_t * jnp.exp2(a_last - r_row)).astype(BF16))
                    cds.append(jnp.exp2(a_last))
                rhs = jnp.concatenate([xb, st.astype(BF16)], axis=0)
                y2 = _dot(jnp.concatenate(lhs_y, axis=0), rhs)
                y_tiles.append(jnp.where(lo, y2[0:L, :], y2[L:2 * L, :])
                               + dsk_ref[:, hp * LANES:(hp + 1) * LANES] * x_pair)
                s2 = _dot(jnp.concatenate(lhs_s, axis=0), xb)
                cd = jnp.where(lo_row, cds[0], cds[1])
                state_ref[hp] = st * cd + jnp.where(lo, s2[0:SSM_STATE, :], s2[SSM_STATE:, :])
            do_side_work(per_stage)
        y = jnp.concatenate(y_tiles, axis=1)
        zc = z_buf[rows, :]
        y = y * (zc * _sigmoid(zc))
        gw = d_inner // SSM_GROUPS
        normed = []
        for g in range(SSM_GROUPS):
            yg = y[:, g * gw:(g + 1) * gw]
            ms = jnp.mean(yg * yg, axis=-1, keepdims=True)
            normed.append(yg * lax.rsqrt(ms + RMS_EPS))
        y_out = (jnp.concatenate(normed, axis=1) * ng_ref[...]).astype(BF16)
        o_ref[rows, :] = hprev_ref[rows, :] + _dot(y_out, wout_ref[...])

    do_side_work(len(side_work))


def _ssm_body(h_ref, hprev_ref, g_ref, wz_ref, wx_ref, wdt_ref, cw_ref, cb_ref, dtb_ref, a_ref,
              dsk_ref, ng_ref, wout_ref, o_ref, z0, x0, d0, z1, x1, d1, state_ref, tail_ref,
              *, tm, tiles_per_seq):
    t = pl.program_id(0)

    @pl.when(t == 0)
    def _():
        z1[...] = jnp.zeros_like(z1)
        x1[...] = jnp.zeros_like(x1)
        d1[...] = jnp.zeros_like(d1)

    @pl.when((t == 0) | ((t - 1) % tiles_per_seq == 0))
    def _():
        state_ref[...] = jnp.zeros_like(state_ref)
        tail_ref[...] = jnp.zeros_like(tail_ref)

    def run(pbuf, mbuf):
        pieces = _ssm_project_pieces(h_ref, g_ref, wz_ref, wx_ref, wdt_ref, *pbuf)
        _ssm_mix(hprev_ref, *mbuf, cw_ref, cb_ref, dtb_ref, a_ref, dsk_ref, ng_ref, wout_ref,
                 o_ref, state_ref, tail_ref, pieces, tm=tm)

    @pl.when(t % 2 == 0)
    def _():
        run((z0, x0, d0), (z1, x1, d1))

    @pl.when(t % 2 == 1)
    def _():
        run((z1, x1, d1), (z0, x0, d0))


def _ssm_fused(h, g, wz, wx, wdt, cw, cb, dtb, a_log, dsk, ng, wout, *, tm, seq):
    t, d = h.shape
    d_inner, conv_w = wz.shape[1], wx.shape[1]
    n_tiles = t // tm
    return pl.pallas_call(
        functools.partial(_ssm_body, tm=tm, tiles_per_seq=seq // tm),
        out_shape=jax.ShapeDtypeStruct((t, d), F32),
        grid=(n_tiles + 1,),
        in_specs=[
            pl.BlockSpec((tm, d), lambda i: (jnp.minimum(i, n_tiles - 1), 0)),
            pl.BlockSpec((tm, d), lambda i: (jnp.maximum(i - 1, 0), 0)),
            _resident((1, d)),
            _resident((d, d_inner)),
            _resident((d, conv_w)),
            _resident((d, LANES)),
            _resident((SSM_CONV, conv_w)),
            _resident((1, conv_w)),
            _resident((1, LANES)),
            _resident((1, LANES)),
            _resident((1, d_inner)),
            _resident((1, d_inner)),
            _resident((d_inner, d)),
        ],
        out_specs=pl.BlockSpec((tm, d), lambda i: (jnp.maximum(i - 1, 0), 0)),
        scratch_shapes=[
            pltpu.VMEM((tm, d_inner), F32), pltpu.VMEM((tm, conv_w), F32), pltpu.VMEM((tm, LANES), F32),
            pltpu.VMEM((tm, d_inner), F32), pltpu.VMEM((tm, conv_w), F32), pltpu.VMEM((tm, LANES), F32),
            pltpu.VMEM((SSM_HEADS // 2, SSM_STATE, LANES), F32),
            pltpu.VMEM((8, conv_w), F32),
        ],
        compiler_params=_cparams(1, sequential=True, vmem_limit=VMEM_LIMIT_SSM),
        name="ssm",
    )(h, h, g, wz, wx, wdt, cw, cb, dtb, a_log, dsk, ng, wout)


def _pad_lanes(v):
    v = v.reshape(1, -1).astype(F32)
    return jnp.pad(v, ((0, 0), (0, LANES - v.shape[1])))


def _swap_heads(w):
    return jnp.concatenate([w[:, HEAD_DIM:], w[:, :HEAD_DIM]], axis=1)


def _attention_layer(h3, g, w_in, forget_bias, swa_qn, swa_kn, sinks, fox_qn, fox_kn, w_out,
                     *, tm_in, tq):
    b, s, d = h3.shape
    qa_w = SWA_Q_HEADS * HEAD_DIM
    kv_w = SWA_KV_HEADS * HEAD_DIM
    fox_w = FOX_HEADS * HEAD_DIM
    o = 0
    w_qa = w_in[:, o:o + qa_w]; o += qa_w
    w_ka = w_in[:, o:o + kv_w]; o += kv_w
    w_va = w_in[:, o:o + kv_w]; o += kv_w
    w_fox = w_in[:, o:o + 3 * fox_w]; o += 3 * fox_w
    w_f = w_in[:, o:]
    w_main = jnp.concatenate(
        [w_qa, w_ka, _swap_heads(w_ka), w_va, _swap_heads(w_va), w_fox], axis=1).astype(BF16)
    w_fp = jnp.pad(w_f, ((0, 0), (0, LANES - w_f.shape[1]))).astype(BF16)
    gains = jnp.stack([jnp.tile(v.astype(F32), 2) for v in (swa_qn, swa_kn, fox_qn, fox_kn)])
    qkv, cum = _attn_in(h3, g.reshape(1, d), w_main, w_fp, _pad_lanes(forget_bias), gains, tm=tm_in)
    o_a = _swa(sinks.astype(F32), qkv)
    nq = s // tq
    cum_row = cum[:, :, :FOX_HEADS].reshape(b, nq, tq, FOX_HEADS // 2, 2).transpose(0, 3, 1, 4, 2)
    o_b = _fox(qkv, cum, cum_row, tq=tq)
    w_out_b = w_out.astype(BF16)
    return (o_a.reshape(b * s, qa_w), o_b.reshape(b * s, fox_w), w_out_b[:qa_w], w_out_b[qa_w:])


def _ssm_layer(h3, g, w_in, conv_w, conv_b, dt_bias, a_log, d_skip, norm_g, w_out,
               *, tm):
    b, s, d = h3.shape
    d_inner = SSM_HEADS * HEAD_DIM
    cw = d_inner + 2 * SSM_GROUPS * SSM_STATE
    w_z = w_in[:, :d_inner].astype(BF16)
    w_x = w_in[:, d_inner:d_inner + cw].astype(BF16)
    w_dt = w_in[:, d_inner + cw:]
    w_dt = jnp.pad(w_dt, ((0, 0), (0, LANES - w_dt.shape[1]))).astype(BF16)
    dsk = jnp.repeat(d_skip.astype(F32), HEAD_DIM).reshape(1, d_inner)
    out = _ssm_fused(h3.reshape(b * s, d), g.reshape(1, d), w_z, w_x, w_dt,
                     conv_w.astype(F32), conv_b.reshape(1, cw).astype(F32), _pad_lanes(dt_bias),
                     _pad_lanes(a_log), dsk, norm_g.reshape(1, d_inner).astype(F32),
                     w_out.astype(BF16), tm=tm, seq=s)
    return out.reshape(b, s, d)


def kernel(x, ffn1_norm, ffn1_w_gate, ffn1_w_up, ffn1_w_down, mix_norm, ffn2_norm, ffn2_w_gate, ffn2_w_up, ffn2_w_down, attn_w_in, attn_forget_bias, swa_q_norm, swa_k_norm, swa_sinks, fox_q_norm, fox_k_norm, attn_w_out, ssm_w_in, ssm_conv_w, ssm_conv_b, ssm_dt_bias, ssm_a_log, ssm_d_skip, ssm_norm, ssm_w_out):
    b, s, d = x.shape
    depth = ffn1_norm.shape[0]
    t = b * s
    tm = min(512, t)
    tq = min(512, s)

    ffn1 = (ffn1_norm.reshape(depth, 1, d), ffn1_w_gate.astype(BF16), ffn1_w_up.astype(BF16),
            ffn1_w_down.astype(BF16))
    ffn2 = (ffn2_norm.reshape(depth, 1, d), ffn2_w_gate.astype(BF16), ffn2_w_up.astype(BF16),
            ffn2_w_down.astype(BF16))

    def ffn(h3, params, layer, mixer_out=None):
        return _ffn(h3.reshape(t, d), *params, layer=layer, tm=tm,
                    mixer_out=mixer_out).reshape(b, s, d)

    h = x
    for layer in range(depth):
        h = ffn(h, ffn1, layer)
        i = layer // 2
        mixer_out = None
        if layer % 2 == 0:
            mixer_out = _attention_layer(h, mix_norm[layer], attn_w_in[i], attn_forget_bias[i],
                                         swa_q_norm[i], swa_k_norm[i], swa_sinks[i],
                                         fox_q_norm[i], fox_k_norm[i], attn_w_out[i],
                                         tm_in=min(512, s), tq=tq)
        else:
            h = _ssm_layer(h, mix_norm[layer], ssm_w_in[i], ssm_conv_w[i], ssm_conv_b[i],
                           ssm_dt_bias[i], ssm_a_log[i], ssm_d_skip[i], ssm_norm[i],
                           ssm_w_out[i], tm=min(256, s))
        h = ffn(h, ffn2, layer, mixer_out)
    return h
```

```python
import functools
import math

import jax
import jax.numpy as jnp
from jax import lax
from jax.experimental import pallas as pl
from jax.experimental.pallas import tpu as pltpu

F32 = jnp.float32
BF16 = jnp.bfloat16

RMS_EPS = 1e-6
HEAD_DIM = 64
LANES = 128
SWA_Q_HEADS = 8
SWA_KV_HEADS = 2
SWA_BLOCK = 128
FOX_HEADS = 8
SSM_HEADS = 32
SSM_GROUPS = 4
SSM_STATE = 128
SSM_CHUNK = 128
SSM_CONV = 4
NEG_BIG = -1e30
LOG2E = math.log2(math.e)

VMEM_LIMIT = 56 * 1024 * 1024


def _cparams(n_axes, sequential=False):
    sem = ("arbitrary",) * n_axes if sequential else ("parallel",) * n_axes
    return pltpu.CompilerParams(dimension_semantics=sem, vmem_limit_bytes=VMEM_LIMIT)


def _resident(shape):
    nd = len(shape)
    return pl.BlockSpec(shape, lambda *_: (0,) * nd, pipeline_mode=pl.Buffered(1))


BF16_SUBLANES = 16


def _cast_jobs(jobs, n_steps, step_of):
    in_specs, out_specs, out_shapes = [], [], []
    for arr, layer in jobs:
        _, rows, cols = arr.shape
        rb = next(r for r in range(BF16_SUBLANES, rows + 1, BF16_SUBLANES)
                  if rows % r == 0 and rows // r <= n_steps)
        last = rows // rb - 1

        def block(*idx, last=last):
            return jnp.minimum(step_of(*idx), last)

        in_specs.append(pl.BlockSpec((None, rb, cols),
                                     lambda *idx, layer=layer, block=block: (layer, block(*idx), 0)))
        out_specs.append(pl.BlockSpec((rb, cols), lambda *idx, block=block: (block(*idx), 0)))
        out_shapes.append(jax.ShapeDtypeStruct((rows, cols), BF16))
    return in_specs, out_specs, out_shapes


def _with_casts(body, n_in, n_out, n_cast):
    def wrapped(*refs):
        ins, rest = refs[:n_in], refs[n_in:]
        srcs, rest = rest[:n_cast], rest[n_cast:]
        outs, rest = rest[:n_out], rest[n_out:]
        dsts, scratch = rest[:n_cast], rest[n_cast:]
        for src, dst in zip(srcs, dsts):
            dst[...] = src[...].astype(BF16)
        body(*ins, *outs, *scratch)
    return wrapped


def _rmsnorm(x, g):
    ms = jnp.mean(x * x, axis=-1, keepdims=True)
    return x * lax.rsqrt(ms + RMS_EPS) * g


def _sigmoid(x):
    return 1.0 / (1.0 + jnp.exp(-x))


def _softplus(x):
    return jnp.maximum(x, 0.0) + jnp.log1p(jnp.exp(-jnp.abs(x)))


def _lane_lo(shape):
    return lax.broadcasted_iota(jnp.int32, shape, len(shape) - 1) < HEAD_DIM


def _head_rmsnorm(x, g):
    lo = _lane_lo(x.shape)
    sq = x * x
    s_lo = jnp.sum(jnp.where(lo, sq, 0.0), axis=-1, keepdims=True)
    s_hi = jnp.sum(jnp.where(lo, 0.0, sq), axis=-1, keepdims=True)
    ms = jnp.where(lo, s_lo, s_hi) * (1.0 / HEAD_DIM)
    return x * lax.rsqrt(ms + RMS_EPS) * g


def _dot(a, b):
    return jnp.dot(a, b, preferred_element_type=F32)


def _dot_nt(a, b):
    return lax.dot_general(a, b, (((1,), (1,)), ((), ())), preferred_element_type=F32)


def _lower_tri(n):
    r = lax.broadcasted_iota(jnp.int32, (n, n), 0)
    c = lax.broadcasted_iota(jnp.int32, (n, n), 1)
    return r >= c


def _cumsum_rows(x, tri_bf, one_matmul):
    hi = x.astype(BF16)
    r1 = x - hi.astype(F32)
    mid = r1.astype(BF16)
    lo = (r1 - mid.astype(F32)).astype(BF16)
    if not one_matmul:
        return _dot(tri_bf, hi) + _dot(tri_bf, mid) + _dot(tri_bf, lo)
    w = x.shape[1]
    parts = _dot(tri_bf, jnp.concatenate([hi, mid, lo], axis=1))
    return parts[:, :w] + parts[:, w:2 * w] + parts[:, 2 * w:]


def _ffn_body(*refs, with_mixer_out):
    if with_mixer_out:
        h_ref, a_ref, b_ref, wa_ref, wb_ref, g_ref, wg_ref, wu_ref, wd_ref, o_ref = refs
        x = h_ref[...] + _dot(a_ref[...], wa_ref[...]) + _dot(b_ref[...], wb_ref[...])
    else:
        h_ref, g_ref, wg_ref, wu_ref, wd_ref, o_ref = refs
        x = h_ref[...]
    xn = _rmsnorm(x, g_ref[...]).astype(BF16)
    gate = _dot(xn, wg_ref[...])
    up = _dot(xn, wu_ref[...])
    act = (gate * _sigmoid(gate) * up).astype(BF16)
    o_ref[...] = x + 0.5 * _dot(act, wd_ref[...])


def _ffn(h, g, wg, wu, wd, *, tm, mixer_out=None):
    t, d = h.shape
    dff = wg.shape[1]
    tile = lambda width: pl.BlockSpec((tm, width), lambda i: (i, 0))
    mix_args, mix_specs = (), []
    if mixer_out is not None:
        o_a, o_b, w_a, w_b = mix_args = mixer_out
        mix_specs = [tile(o_a.shape[1]), tile(o_b.shape[1]), _resident(w_a.shape), _resident(w_b.shape)]
    return pl.pallas_call(
        functools.partial(_ffn_body, with_mixer_out=mixer_out is not None),
        out_shape=jax.ShapeDtypeStruct((t, d), F32),
        grid=(t // tm,),
        in_specs=[tile(d)] + mix_specs + [
            _resident((1, d)),
            _resident((d, dff)),
            _resident((d, dff)),
            _resident((dff, d)),
        ],
        out_specs=tile(d),
        compiler_params=_cparams(1),
        name="ffn",
    )(h, *mix_args, g, wg, wu, wd)


ATT_NORM_Q_SWA = (0, 1, 2, 3)
ATT_NORM_K_SWA = (4, 5)
ATT_COPY = (6, 7, 16, 17, 18, 19)
ATT_NORM_Q_FOX = (8, 9, 10, 11)
ATT_NORM_K_FOX = (12, 13, 14, 15)
ATT_TILES = 20


def _attn_in_body(h_ref, g_ref, w_ref, wf_ref, fb_ref, gains_ref, qkv_ref, cum_ref, carry_ref):
    i = pl.program_id(1)

    @pl.when(i == 0)
    def _():
        carry_ref[...] = jnp.zeros_like(carry_ref)

    x = h_ref[...]
    u = _rmsnorm(x, g_ref[...]).astype(BF16)
    scale = HEAD_DIM ** -0.5 * LOG2E

    fl = _dot(u, wf_ref[...]) + fb_ref[...]
    log_f = -_softplus(-fl) * LOG2E
    tri_bf = _lower_tri(LANES).astype(BF16)
    carry = carry_ref[0:1, :]
    for c0 in range(0, log_f.shape[0], LANES):
        cum = _cumsum_rows(log_f[c0:c0 + LANES, :], tri_bf, one_matmul=True) + carry
        cum_ref[c0:c0 + LANES, :] = cum
        carry = cum[LANES - 1:LANES, :]
    carry_ref[0:1, :] = carry

    for k0 in range(0, ATT_TILES, 2):
        proj = _dot(u, w_ref[:, k0 * LANES:(k0 + 2) * LANES])
        for k in (k0, k0 + 1):
            t = proj[:, (k - k0) * LANES:(k - k0 + 1) * LANES]
            if k in ATT_NORM_Q_SWA:
                t = _head_rmsnorm(t, gains_ref[0:1, :]) * scale
            elif k in ATT_NORM_K_SWA:
                t = _head_rmsnorm(t, gains_ref[1:2, :])
            elif k in ATT_NORM_Q_FOX:
                t = _head_rmsnorm(t, gains_ref[2:3, :]) * scale
            elif k in ATT_NORM_K_FOX:
                t = _head_rmsnorm(t, gains_ref[3:4, :])
            qkv_ref[:, k * LANES:(k + 1) * LANES] = t.astype(BF16)


def _attn_in(h, g, w, wf, fb, gains, *, tm):
    b, s, d = h.shape
    n = w.shape[1]
    return pl.pallas_call(
        _attn_in_body,
        out_shape=(jax.ShapeDtypeStruct((b, s, n), BF16),
                   jax.ShapeDtypeStruct((b, s, LANES), F32)),
        grid=(b, s // tm),
        in_specs=[
            pl.BlockSpec((None, tm, d), lambda bi, i: (bi, i, 0)),
            _resident((1, d)),
            _resident((d, n)),
            _resident((d, LANES)),
            _resident((1, LANES)),
            _resident((4, LANES)),
        ],
        out_specs=(pl.BlockSpec((None, tm, n), lambda bi, i: (bi, i, 0)),
                   pl.BlockSpec((None, tm, LANES), lambda bi, i: (bi, i, 0))),
        scratch_shapes=[pltpu.VMEM((8, LANES), F32)],
        compiler_params=_cparams(2, sequential=True),
        name="attn_in",
    )(h, g, w, wf, fb, gains)


def _swa_block(sinks_ref, q, kv, mask):
    blk = SWA_BLOCK
    nkeys = kv.shape[0]
    lo = _lane_lo((blk, LANES))
    ones = jnp.ones((nkeys, LANES), BF16)
    mask2 = jnp.concatenate([mask, mask], axis=0)
    first_head_rows = lax.broadcasted_iota(jnp.int32, (2 * blk, 1), 0) < blk
    outs = {}
    for kv_head in range(SWA_KV_HEADS):
        tiles = (2 * kv_head, 2 * kv_head + 1)
        for j in range(2):
            var = 0 if kv_head == j else 1
            kk = kv[:, var * LANES:(var + 1) * LANES]
            v_aug = jnp.concatenate([kv[:, (2 + var) * LANES:(3 + var) * LANES], ones], axis=1)
            sel = lo if j == 0 else ~lo
            qm = jnp.concatenate(
                [jnp.where(sel, q[:, t * LANES:(t + 1) * LANES], jnp.zeros((blk, LANES), BF16))
                 for t in tiles], axis=0)
            sink = jnp.where(first_head_rows, sinks_ref[2 * tiles[0] + j] * LOG2E,
                             sinks_ref[2 * tiles[1] + j] * LOG2E)
            sc = jnp.where(mask2, _dot_nt(qm, kk), -jnp.inf)
            m = jnp.maximum(jnp.max(sc, axis=-1, keepdims=True), sink)
            pv = _dot(jnp.exp2(sc - m).astype(BF16), v_aug)
            o = pv[:, :LANES] / (pv[:, LANES:] + jnp.exp2(sink - m))
            outs[tiles[0], j] = o[:blk, :]
            outs[tiles[1], j] = o[blk:, :]
    return jnp.concatenate([jnp.where(lo, outs[t, 0], outs[t, 1])
                            for t in range(SWA_Q_HEADS // 2)], axis=1).astype(BF16)


def _swa_body(sinks_ref, q_ref, kv_ref, o_ref):
    blk = SWA_BLOCK
    r = lax.broadcasted_iota(jnp.int32, (blk, 2 * blk), 0)
    c = lax.broadcasted_iota(jnp.int32, (blk, 2 * blk), 1)
    o_ref[0:blk, :] = _swa_block(sinks_ref, q_ref[0:blk, :], kv_ref[0:blk, :], _lower_tri(blk))
    window = (c > r) & (c <= r + blk)

    def step(n, carry):
        q0 = pl.multiple_of(n * blk, blk)
        k0 = pl.multiple_of((n - 1) * blk, blk)
        o_ref[pl.ds(q0, blk), :] = _swa_block(sinks_ref, q_ref[pl.ds(q0, blk), :],
                                              kv_ref[pl.ds(k0, 2 * blk), :], window)
        return carry

    lax.fori_loop(1, q_ref.shape[0] // blk, step, 0)


def _swa(sinks, qkv):
    b, s, _ = qkv.shape
    qw = SWA_Q_HEADS * HEAD_DIM
    return pl.pallas_call(
        _swa_body,
        out_shape=jax.ShapeDtypeStruct((b, s, qw), BF16),
        grid_spec=pltpu.PrefetchScalarGridSpec(
            num_scalar_prefetch=1,
            grid=(b,),
            in_specs=[
                pl.BlockSpec((None, s, qw), lambda bi, sk: (bi, 0, 0)),
                pl.BlockSpec((None, s, qw), lambda bi, sk: (bi, 0, 1)),
            ],
            out_specs=pl.BlockSpec((None, s, qw), lambda bi, sk: (bi, 0, 0)),
        ),
        compiler_params=_cparams(1),
        name="swa",
    )(sinks, qkv, qkv)


FOX_Q_TILE0 = 8
FOX_K_TILE0 = 12
FOX_V_TILE0 = 16


def _fox_body(q_ref, k_ref, v_ref, fq_ref, fk_ref, o_ref, m_ref, acc_ref, sa_ref, sb_ref, *, tq):
    hp = pl.program_id(1)
    i = pl.program_id(2)
    q = q_ref[...]
    fq_all = fq_ref[...]
    lo = _lane_lo((tq, LANES))
    lane = lax.broadcasted_iota(jnp.int32, (tq, LANES), 1)
    zero = jnp.zeros_like(q)
    q2 = jnp.concatenate([jnp.where(lo, q, zero), jnp.where(lo, zero, q)], axis=0)
    fq0 = jnp.sum(jnp.where(lane == 2 * hp, fq_all, 0.0), axis=-1, keepdims=True)
    fq1 = jnp.sum(jnp.where(lane == 2 * hp + 1, fq_all, 0.0), axis=-1, keepdims=True)
    ones = jnp.ones((tq, LANES), BF16)

    m_ref[...] = jnp.full(m_ref.shape, NEG_BIG, F32)
    acc_ref[...] = jnp.zeros(acc_ref.shape, F32)

    def scores(jb, masked):
        start = pl.multiple_of(jb * tq, tq)
        kb = k_ref[pl.ds(start, tq), :]
        fk = fk_ref[jb]
        sc = _dot_nt(q2, kb) + jnp.concatenate([fq0 - fk[0:1, :], fq1 - fk[1:2, :]], axis=0)
        if masked:
            tri = _lower_tri(tq)
            sc = jnp.where(jnp.concatenate([tri, tri], axis=0), sc, NEG_BIG)
        return sc

    def accumulate(sc, jb):
        start = pl.multiple_of(jb * tq, tq)
        v_aug = jnp.concatenate([v_ref[pl.ds(start, tq), :], ones], axis=1)
        m_old = m_ref[...]
        m_new = jnp.maximum(m_old, jnp.max(sc, axis=-1, keepdims=True))
        alpha = jnp.exp2(m_old - m_new)
        p = jnp.concatenate([jnp.exp2(sc[:, t * LANES:(t + 1) * LANES] - m_new)
                             for t in range(tq // LANES)], axis=1).astype(BF16)
        pv = _dot(p, v_aug)
        acc_ref[:, :LANES] = alpha * acc_ref[:, :LANES] + pv[:, :LANES]
        acc_ref[:, LANES:] = alpha * acc_ref[:, LANES:] + pv[:, LANES:]
        m_ref[...] = m_new

    sa_ref[...] = scores(i, True)

    def visited(n):
        return jnp.where(n == 0, i, n - 1)

    def pair(u, carry):
        sb_ref[...] = scores(2 * u, False)
        accumulate(sa_ref[...], visited(2 * u))
        sa_ref[...] = scores(2 * u + 1, False)
        accumulate(sb_ref[...], 2 * u)
        return carry

    lax.fori_loop(0, i // 2, pair, 0)

    @pl.when(i % 2 == 1)
    def _():
        sb_ref[...] = scores(i - 1, False)
        accumulate(sa_ref[...], visited(i - 1))
        accumulate(sb_ref[...], i - 1)

    @pl.when(i % 2 == 0)
    def _():
        accumulate(sa_ref[...], visited(i))

    acc = acc_ref[...]
    o = acc[:, :LANES] / acc[:, LANES:]
    o_ref[...] = jnp.where(lo, o[:tq, :], o[tq:, :]).astype(BF16)


def _fox(qkv, cum_col, cum_row, cast_jobs, *, tq):
    b, s, _ = qkv.shape
    nq = s // tq
    npairs = FOX_HEADS // 2
    cast_in, cast_out, cast_shapes = _cast_jobs(
        cast_jobs, b * npairs * nq, lambda bi, hp, i: (bi * npairs + hp) * nq + i)
    in_specs = [
        pl.BlockSpec((None, tq, LANES), lambda bi, hp, i: (bi, i, FOX_Q_TILE0 + hp)),
        pl.BlockSpec((None, s, LANES), lambda bi, hp, i: (bi, 0, FOX_K_TILE0 + hp)),
        pl.BlockSpec((None, s, LANES), lambda bi, hp, i: (bi, 0, FOX_V_TILE0 + hp)),
        pl.BlockSpec((None, tq, LANES), lambda bi, hp, i: (bi, i, 0)),
        pl.BlockSpec((None, None, nq, 2, tq), lambda bi, hp, i: (bi, hp, 0, 0, 0)),
    ]
    out, *casts = pl.pallas_call(
        _with_casts(functools.partial(_fox_body, tq=tq), len(in_specs), 1, len(cast_jobs)),
        out_shape=[jax.ShapeDtypeStruct((b, s, FOX_HEADS * HEAD_DIM), BF16)] + cast_shapes,
        grid=(b, npairs, nq),
        in_specs=in_specs + cast_in,
        out_specs=[pl.BlockSpec((None, tq, LANES), lambda bi, hp, i: (bi, i, hp))] + cast_out,
        scratch_shapes=[pltpu.VMEM((2 * tq, LANES), F32), pltpu.VMEM((2 * tq, 2 * LANES), F32),
                        pltpu.VMEM((2 * tq, tq), F32), pltpu.VMEM((2 * tq, tq), F32)],
        compiler_params=_cparams(3, sequential=True),
        name="fox",
    )(qkv, qkv, qkv, cum_col, cum_row, *(arr for arr, _ in cast_jobs))
    return out, casts


SSM_CONV_BLOCK = 512
SSM_PROJ_BLOCK = 256


def _ssm_project_pieces(h_ref, g_ref, wz_ref, wx_ref, wdt_ref, z_buf, x_buf, dt_buf):
    u = _rmsnorm(h_ref[...], g_ref[...]).astype(BF16)

    def piece(buf, w_ref, c0, c1):
        def run():
            buf[:, c0:c1] = _dot(u, w_ref[:, c0:c1])
        return run

    pieces = [piece(dt_buf, wdt_ref, 0, LANES)]
    for buf, w_ref in ((x_buf, wx_ref), (z_buf, wz_ref)):
        n = w_ref.shape[1]
        pieces += [piece(buf, w_ref, c0, min(c0 + SSM_PROJ_BLOCK, n))
                   for c0 in range(0, n, SSM_PROJ_BLOCK)]
    return pieces


def _ssm_mix(hprev_ref, z_buf, x_buf, dt_buf, cw_ref, cb_ref, dtb_ref, a_ref, dsk_ref, ng_ref,
             wout_ref, o_ref, state_ref, tail_ref, side_work, *, tm):
    d_inner = SSM_HEADS * HEAD_DIM
    bc_w = SSM_GROUPS * SSM_STATE
    L = SSM_CHUNK
    stages = (tm // L) * SSM_GROUPS
    per_stage = -(-len(side_work) // stages)

    def do_side_work(n):
        for _ in range(n):
            if side_work:
                side_work.pop(0)()

    conv_w = x_buf.shape[1]
    row8 = lax.broadcasted_iota(jnp.int32, (8, SSM_CONV_BLOCK), 0)
    xbc_blocks = []
    for c0 in range(0, conv_w, SSM_CONV_BLOCK):
        cols = slice(c0, c0 + SSM_CONV_BLOCK)
        cur = x_buf[:, cols]
        prev = tail_ref[:, cols]
        conv = cur * cw_ref[SSM_CONV - 1:SSM_CONV, cols] + cb_ref[:, cols]
        for sh in range(1, SSM_CONV):
            cur_s = pltpu.roll(cur, sh, axis=0)
            prev_s = pltpu.roll(prev, sh, axis=0)
            top = jnp.where(row8 < sh, prev_s, cur_s[0:8, :])
            shifted = jnp.concatenate([top, cur_s[8:, :]], axis=0)
            conv = conv + shifted * cw_ref[SSM_CONV - 1 - sh:SSM_CONV - sh, cols]
        tail_ref[:, cols] = cur[tm - 8:tm, :]
        xbc_blocks.append(conv * _sigmoid(conv))
    xbc =jnp.concatenate(xbc_blocks, axis=1)

    a2 = -jnp.exp(a_ref[...]) * LOG2E
    dt_all = _softplus(dt_buf[...] + dtb_ref[...])
    tri = _lower_tri(L)
    tri_bf = tri.astype(BF16)
    lo = _lane_lo((L, LANES))
    lo_row = _lane_lo((1, LANES))
    heads_per_group = SSM_HEADS // SSM_GROUPS

    for c in range(tm // L):
        rows = slice(c * L, (c + 1) * L)
        dt = dt_all[rows, :]
        a_cs = _cumsum_rows(dt * a2, tri_bf, one_matmul=False)
        a_cs_t = a_cs.T
        r_t = a_cs_t - jnp.log2(dt.T)
        y_tiles = []
        for g in range(SSM_GROUPS):
            b_g = xbc[rows, d_inner + g * SSM_STATE:d_inner + (g + 1) * SSM_STATE]
            c_g = xbc[rows, d_inner + bc_w + g * SSM_STATE:d_inner + bc_w + (g + 1) * SSM_STATE]
            cb = _dot_nt(c_g.astype(BF16), b_g.astype(BF16))
            b_t = b_g.T
            for pr in range(heads_per_group // 2):
                hp = g * (heads_per_group // 2) + pr
                x_pair = xbc[rows, hp * LANES:(hp + 1) * LANES]
                xb = x_pair.astype(BF16)
                st = state_ref[hp]
                lhs_y, lhs_s, cds = [], [], []
                for j in range(2):
                    h = 2 * hp + j
                    a_col = jnp.broadcast_to(a_cs[:, h:h + 1], (L, L))
                    r_row = r_t[h:h + 1, :]
                    a_last = a_cs_t[h:h + 1, L - 1:L]
                    m_diag = cb * jnp.exp2(jnp.where(tri, a_col - r_row, -jnp.inf))
                    c_off = c_g * jnp.exp2(a_col)
                    lhs_y.append(jnp.concatenate([m_diag, c_off], axis=1).astype(BF16))
                    lhs_s.append((b_t * jnp.exp2(a_last - r_row)).astype(BF16))
                    cds.append(jnp.exp2(a_last))
                rhs = jnp.concatenate([xb, st.astype(BF16)], axis=0)
                y2 = _dot(jnp.concatenate(lhs_y, axis=0), rhs)
                y_tiles.append(jnp.where(lo, y2[0:L, :], y2[L:2 * L, :])
                               + dsk_ref[:, hp * LANES:(hp + 1) * LANES] * x_pair)
                s2 = _dot(jnp.concatenate(lhs_s, axis=0), xb)
                cd = jnp.where(lo_row, cds[0], cds[1])
                state_ref[hp] = st * cd + jnp.where(lo, s2[0:SSM_STATE, :], s2[SSM_STATE:, :])
            do_side_work(per_stage)
        y = jnp.concatenate(y_tiles, axis=1)
        zc = z_buf[rows, :]
        y = y * (zc * _sigmoid(zc))
        gw = d_inner // SSM_GROUPS
        normed = []
        for g in range(SSM_GROUPS):
            yg = y[:, g * gw:(g + 1) * gw]
            ms = jnp.mean(yg * yg, axis=-1, keepdims=True)
            normed.append(yg * lax.rsqrt(ms + RMS_EPS))
        y_out = (jnp.concatenate(normed, axis=1) * ng_ref[...]).astype(BF16)
        o_ref[rows, :] = hprev_ref[rows, :] + _dot(y_out, wout_ref[...])

    do_side_work(len(side_work))


def _ssm_body(h_ref, hprev_ref, g_ref, wz_ref, wx_ref, wdt_ref, cw_ref, cb_ref, dtb_ref, a_ref,
              dsk_ref, ng_ref, wout_ref, o_ref, z0, x0, d0, z1, x1, d1, state_ref, tail_ref,
              *, tm, tiles_per_seq):
    t = pl.program_id(0)

    @pl.when(t == 0)
    def _():
        z1[...] = jnp.zeros_like(z1)
        x1[...] = jnp.zeros_like(x1)
        d1[...] = jnp.zeros_like(d1)

    @pl.when((t == 0) | ((t - 1) % tiles_per_seq == 0))
    def _():
        state_ref[...] = jnp.zeros_like(state_ref)
        tail_ref[...] = jnp.zeros_like(tail_ref)

    def run(pbuf, mbuf):
        pieces = _ssm_project_pieces(h_ref, g_ref, wz_ref, wx_ref, wdt_ref, *pbuf)
        _ssm_mix(hprev_ref, *mbuf, cw_ref, cb_ref, dtb_ref, a_ref, dsk_ref, ng_ref, wout_ref,
                 o_ref, state_ref, tail_ref, pieces, tm=tm)

    @pl.when(t % 2 == 0)
    def _():
        run((z0, x0, d0), (z1, x1, d1))

    @pl.when(t % 2 == 1)
    def _():
        run((z1, x1, d1), (z0, x0, d0))


def _ssm_fused(h, g, wz, wx, wdt, cw, cb, dtb, a_log, dsk, ng, wout, cast_jobs, *, tm, seq):
    t, d = h.shape
    d_inner, conv_w = wz.shape[1], wx.shape[1]
    n_tiles = t // tm
    cast_in, cast_out, cast_shapes = _cast_jobs(cast_jobs, n_tiles + 1, lambda i: i)
    in_specs = [
        pl.BlockSpec((tm, d), lambda i: (jnp.minimum(i, n_tiles - 1), 0)),
        pl.BlockSpec((tm, d), lambda i: (jnp.maximum(i - 1, 0), 0)),
        _resident((1, d)),
        _resident((d, d_inner)),
        _resident((d, conv_w)),
        _resident((d, LANES)),
        _resident((SSM_CONV, conv_w)),
        _resident((1, conv_w)),
        _resident((1, LANES)),
        _resident((1, LANES)),
        _resident((1, d_inner)),
        _resident((1, d_inner)),
        _resident((d_inner, d)),
    ]
    out, *casts = pl.pallas_call(
        _with_casts(functools.partial(_ssm_body, tm=tm, tiles_per_seq=seq // tm),
                    len(in_specs), 1, len(cast_jobs)),
        out_shape=[jax.ShapeDtypeStruct((t, d), F32)] + cast_shapes,
        grid=(n_tiles + 1,),
        in_specs=in_specs + cast_in,
        out_specs=[pl.BlockSpec((tm, d), lambda i: (jnp.maximum(i - 1, 0), 0))] + cast_out,
        scratch_shapes=[
            pltpu.VMEM((tm, d_inner), F32), pltpu.VMEM((tm, conv_w), F32), pltpu.VMEM((tm, LANES), F32),
            pltpu.VMEM((tm, d_inner), F32), pltpu.VMEM((tm, conv_w), F32), pltpu.VMEM((tm, LANES), F32),
            pltpu.VMEM((SSM_HEADS // 2, SSM_STATE, LANES), F32),
            pltpu.VMEM((8, conv_w), F32),
        ],
        compiler_params=_cparams(1, sequential=True),
        name="ssm",
    )(h, h, g, wz, wx, wdt, cw, cb, dtb, a_log, dsk, ng, wout, *(arr for arr, _ in cast_jobs))
    return out, casts


def _pad_lanes(v):
    v = v.reshape(1, -1).astype(F32)
    return jnp.pad(v, ((0, 0), (0, LANES - v.shape[1])))


def _swap_heads(w):
    return jnp.concatenate([w[:, HEAD_DIM:], w[:, :HEAD_DIM]], axis=1)


def _attention_layer(h3, g, w_in, forget_bias, swa_qn, swa_kn, sinks, fox_qn, fox_kn, w_out,
                     cast_jobs, *, tm_in, tq):
    b, s, d = h3.shape
    qa_w = SWA_Q_HEADS * HEAD_DIM
    kv_w = SWA_KV_HEADS * HEAD_DIM
    fox_w = FOX_HEADS * HEAD_DIM
    o = 0
    w_qa = w_in[:, o:o + qa_w]; o += qa_w
    w_ka = w_in[:, o:o + kv_w]; o += kv_w
    w_va = w_in[:, o:o + kv_w]; o += kv_w
    w_fox = w_in[:, o:o + 3 * fox_w]; o += 3 * fox_w
    w_f = w_in[:, o:]
    w_main = jnp.concatenate(
        [w_qa, w_ka, _swap_heads(w_ka), w_va, _swap_heads(w_va), w_fox], axis=1).astype(BF16)
    w_fp = jnp.pad(w_f, ((0, 0), (0, LANES - w_f.shape[1]))).astype(BF16)
    gains = jnp.stack([jnp.tile(v.astype(F32), 2) for v in (swa_qn, swa_kn, fox_qn, fox_kn)])
    qkv, cum = _attn_in(h3, g.reshape(1, d), w_main, w_fp, _pad_lanes(forget_bias), gains, tm=tm_in)
    o_a = _swa(sinks.astype(F32), qkv)
    nq = s // tq
    cum_row = cum[:, :, :FOX_HEADS].reshape(b, nq, tq, FOX_HEADS // 2, 2).transpose(0, 3, 1, 4, 2)
    o_b, casts = _fox(qkv, cum, cum_row, cast_jobs, tq=tq)
    w_out_b = w_out.astype(BF16)
    mixer_out = (o_a.reshape(b * s, qa_w), o_b.reshape(b * s, fox_w), w_out_b[:qa_w], w_out_b[qa_w:])
    return mixer_out, casts


def _ssm_layer(h3, g, w_in, conv_w, conv_b, dt_bias, a_log, d_skip, norm_g, w_out,
               cast_jobs, *, tm):
    b, s, d = h3.shape
    d_inner = SSM_HEADS * HEAD_DIM
    cw = d_inner + 2 * SSM_GROUPS * SSM_STATE
    w_z = w_in[:, :d_inner].astype(BF16)
    w_x = w_in[:, d_inner:d_inner + cw].astype(BF16)
    w_dt = w_in[:, d_inner + cw:]
    w_dt = jnp.pad(w_dt, ((0, 0), (0, LANES - w_dt.shape[1]))).astype(BF16)
    dsk = jnp.repeat(d_skip.astype(F32), HEAD_DIM).reshape(1, d_inner)
    out, casts = _ssm_fused(h3.reshape(b * s, d), g.reshape(1, d), w_z, w_x, w_dt,
                            conv_w.astype(F32), conv_b.reshape(1, cw).astype(F32),
                            _pad_lanes(dt_bias), _pad_lanes(a_log), dsk,
                            norm_g.reshape(1, d_inner).astype(F32), w_out.astype(BF16),
                            cast_jobs, tm=tm, seq=s)
    return out.reshape(b, s, d), casts


def kernel(x, ffn1_norm, ffn1_w_gate, ffn1_w_up, ffn1_w_down, mix_norm, ffn2_norm, ffn2_w_gate, ffn2_w_up, ffn2_w_down, attn_w_in, attn_forget_bias, swa_q_norm, swa_k_norm, swa_sinks, fox_q_norm, fox_k_norm, attn_w_out, ssm_w_in, ssm_conv_w, ssm_conv_b, ssm_dt_bias, ssm_a_log, ssm_d_skip, ssm_norm, ssm_w_out):
    b, s, d = x.shape
    depth = ffn1_norm.shape[0]
    t = b * s
    tm = min(512, t)
    tq = min(512, s)

    ffn1_w = (ffn1_w_gate, ffn1_w_up, ffn1_w_down)
    ffn2_w = (ffn2_w_gate, ffn2_w_up, ffn2_w_down)

    def ffn(h3, g, weights, mixer_out=None):
        return _ffn(h3.reshape(t, d), g.reshape(1, d), *weights, tm=tm,
                    mixer_out=mixer_out).reshape(b, s, d)

    w1 = [w[0].astype(BF16) for w in ffn1_w]
    h = x
    for layer in range(depth):
        h = ffn(h, ffn1_norm[layer], w1)
        cast_jobs = [(w, layer) for w in ffn2_w]
        if layer + 1 < depth:
            cast_jobs += [(w, layer + 1) for w in ffn1_w]
        i = layer // 2
        mixer_out = None
        if layer % 2 == 0:
            mixer_out, casts = _attention_layer(
                h, mix_norm[layer], attn_w_in[i], attn_forget_bias[i], swa_q_norm[i],
                swa_k_norm[i], swa_sinks[i], fox_q_norm[i], fox_k_norm[i], attn_w_out[i],
                cast_jobs, tm_in=min(512, s), tq=tq)
        else:
            h, casts = _ssm_layer(h, mix_norm[layer], ssm_w_in[i], ssm_conv_w[i], ssm_conv_b[i],
                                  ssm_dt_bias[i], ssm_a_log[i], ssm_d_skip[i], ssm_norm[i],
                                  ssm_w_out[i], cast_jobs, tm=min(256, s))
        w2, w1 = casts[:3], casts[3:]
        h = ffn(h, ffn2_norm[layer], w2, mixer_out)
    return h
```

```python
import functools
import math

import jax
import jax.numpy as jnp
from jax import lax
from jax.experimental import pallas as pl
from jax.experimental.pallas import tpu as pltpu

F32 = jnp.float32
BF16 = jnp.bfloat16

RMS_EPS = 1e-6
HEAD_DIM = 64
LANES = 128
SWA_Q_HEADS = 8
SWA_KV_HEADS = 2
SWA_BLOCK = 128
FOX_HEADS = 8
SSM_HEADS = 32
SSM_GROUPS = 4
SSM_STATE = 128
SSM_CHUNK = 128
SSM_CONV = 4
NEG_BIG = -1e30
LOG2E = math.log2(math.e)

VMEM_LIMIT = 56 * 1024 * 1024


def _cparams(n_axes, sequential=False):
    sem = ("arbitrary",) * n_axes if sequential else ("parallel",) * n_axes
    return pltpu.CompilerParams(dimension_semantics=sem, vmem_limit_bytes=VMEM_LIMIT)


def _resident(shape):
    nd = len(shape)
    return pl.BlockSpec(shape, lambda *_: (0,) * nd, pipeline_mode=pl.Buffered(1))


BF16_SUBLANES = 16
CAST_MAX_BLOCKS = 32


def _cast_jobs(jobs, n_steps, step_of):
    in_specs, out_specs, out_shapes, n_blocks = [], [], [], []
    for arr, layer in jobs:
        _, rows, cols = arr.shape
        rb = next(r for r in range(BF16_SUBLANES, rows + 1, BF16_SUBLANES)
                  if rows % r == 0 and rows // r <= min(n_steps, CAST_MAX_BLOCKS))
        n_blocks.append(rows // rb)
        last = rows // rb - 1

        def block(*idx, last=last):
            return jnp.minimum(step_of(*idx), last)

        in_specs.append(pl.BlockSpec((None, rb, cols),
                                     lambda *idx, layer=layer, block=block: (layer, block(*idx), 0)))
        out_specs.append(pl.BlockSpec((rb, cols), lambda *idx, block=block: (block(*idx), 0)))
        out_shapes.append(jax.ShapeDtypeStruct((rows, cols), BF16))
    return in_specs, out_specs, out_shapes, n_blocks


def _with_casts(body, n_in, n_out, cast_blocks, step_of, n_axes):
    n_cast = len(cast_blocks)

    def wrapped(*refs):
        ins, rest = refs[:n_in], refs[n_in:]
        srcs, rest = rest[:n_cast], rest[n_cast:]
        outs, rest = rest[:n_out], rest[n_out:]
        dsts, scratch = rest[:n_cast], rest[n_cast:]
        step = step_of(*(pl.program_id(a) for a in range(n_axes)))
        for src, dst, n_blocks in zip(srcs, dsts, cast_blocks):
            @pl.when(step < n_blocks)
            def _(src=src, dst=dst):
                dst[...] = src[...].astype(BF16)
        body(*ins, *outs, *scratch)
    return wrapped


def _rmsnorm(x, g):
    ms = jnp.mean(x * x, axis=-1, keepdims=True)
    return x * lax.rsqrt(ms + RMS_EPS) * g


def _sigmoid(x):
    return 1.0 / (1.0 + jnp.exp(-x))


def _softplus(x):
    return jnp.maximum(x, 0.0) + jnp.log1p(jnp.exp(-jnp.abs(x)))


def _lane_lo(shape):
    return lax.broadcasted_iota(jnp.int32, shape, len(shape) - 1) < HEAD_DIM


def _head_rmsnorm(x, g):
    lo = _lane_lo(x.shape)
    sq = x * x
    s_lo = jnp.sum(jnp.where(lo, sq, 0.0), axis=-1, keepdims=True)
    s_hi = jnp.sum(jnp.where(lo, 0.0, sq), axis=-1, keepdims=True)
    ms = jnp.where(lo, s_lo, s_hi) * (1.0 / HEAD_DIM)
    return x * lax.rsqrt(ms + RMS_EPS) * g


def _dot(a, b):
    return jnp.dot(a, b, preferred_element_type=F32)


def _dot_nt(a, b):
    return lax.dot_general(a, b, (((1,), (1,)), ((), ())), preferred_element_type=F32)


def _lower_tri(n):
    r = lax.broadcasted_iota(jnp.int32, (n, n), 0)
    c = lax.broadcasted_iota(jnp.int32, (n, n), 1)
    return r >= c


def _cumsum_rows(x, tri_bf, one_matmul):
    hi = x.astype(BF16)
    r1 = x - hi.astype(F32)
    mid = r1.astype(BF16)
    lo = (r1 - mid.astype(F32)).astype(BF16)
    if not one_matmul:
        return _dot(tri_bf, hi) + _dot(tri_bf, mid) + _dot(tri_bf, lo)
    w = x.shape[1]
    parts = _dot(tri_bf, jnp.concatenate([hi, mid, lo], axis=1))
    return parts[:, :w] + parts[:, w:2 * w] + parts[:, 2 * w:]


def _ffn_body(*refs, with_mixer_out):
    if with_mixer_out:
        h_ref, a_ref, b_ref, wa_ref, wb_ref, g_ref, wg_ref, wu_ref, wd_ref, o_ref = refs
        x = h_ref[...] + _dot(a_ref[...], wa_ref[...]) + _dot(b_ref[...], wb_ref[...])
    else:
        h_ref, g_ref, wg_ref, wu_ref, wd_ref, o_ref = refs
        x = h_ref[...]
    xn = _rmsnorm(x, g_ref[...]).astype(BF16)
    gate = _dot(xn, wg_ref[...])
    up = _dot(xn, wu_ref[...])
    act = (gate * _sigmoid(gate) * up).astype(BF16)
    o_ref[...] = x + 0.5 * _dot(act, wd_ref[...])


def _ffn(h, g, wg, wu, wd, *, tm, mixer_out=None):
    t, d = h.shape
    dff = wg.shape[1]
    tile = lambda width: pl.BlockSpec((tm, width), lambda i: (i, 0))
    mix_args, mix_specs = (), []
    if mixer_out is not None:
        o_a, o_b, w_a, w_b = mix_args = mixer_out
        mix_specs = [tile(o_a.shape[1]), tile(o_b.shape[1]), _resident(w_a.shape), _resident(w_b.shape)]
    return pl.pallas_call(
        functools.partial(_ffn_body, with_mixer_out=mixer_out is not None),
        out_shape=jax.ShapeDtypeStruct((t, d), F32),
        grid=(t // tm,),
        in_specs=[tile(d)] + mix_specs + [
            _resident((1, d)),
            _resident((d, dff)),
            _resident((d, dff)),
            _resident((dff, d)),
        ],
        out_specs=tile(d),
        compiler_params=_cparams(1),
        name="ffn",
    )(h, *mix_args, g, wg, wu, wd)


ATT_NORM_Q_SWA = (0, 1, 2, 3)
ATT_NORM_K_SWA = (4, 5)
ATT_COPY = (6, 7, 16, 17, 18, 19)
ATT_NORM_Q_FOX = (8, 9, 10, 11)
ATT_NORM_K_FOX = (12, 13, 14, 15)
ATT_TILES = 20


def _attn_in_body(h_ref, g_ref, w_ref, wf_ref, fb_ref, gains_ref, qkv_ref, cum_ref, carry_ref):
    i = pl.program_id(1)

    @pl.when(i == 0)
    def _():
        carry_ref[...] = jnp.zeros_like(carry_ref)

    x = h_ref[...]
    u = _rmsnorm(x, g_ref[...]).astype(BF16)
    scale = HEAD_DIM ** -0.5 * LOG2E

    fl = _dot(u, wf_ref[...]) + fb_ref[...]
    log_f = -_softplus(-fl) * LOG2E
    tri_bf = _lower_tri(LANES).astype(BF16)
    carry = carry_ref[0:1, :]
    for c0 in range(0, log_f.shape[0], LANES):
        cum = _cumsum_rows(log_f[c0:c0 + LANES, :], tri_bf, one_matmul=True) + carry
        cum_ref[c0:c0 + LANES, :] = cum
        carry = cum[LANES - 1:LANES, :]
    carry_ref[0:1, :] = carry

    for k0 in range(0, ATT_TILES, 2):
        proj = _dot(u, w_ref[:, k0 * LANES:(k0 + 2) * LANES])
        for k in (k0, k0 + 1):
            t = proj[:, (k - k0) * LANES:(k - k0 + 1) * LANES]
            if k in ATT_NORM_Q_SWA:
                t = _head_rmsnorm(t, gains_ref[0:1, :]) * scale
            elif k in ATT_NORM_K_SWA:
                t = _head_rmsnorm(t, gains_ref[1:2, :])
            elif k in ATT_NORM_Q_FOX:
                t = _head_rmsnorm(t, gains_ref[2:3, :]) * scale
            elif k in ATT_NORM_K_FOX:
                t = _head_rmsnorm(t, gains_ref[3:4, :])
            qkv_ref[:, k * LANES:(k + 1) * LANES] = t.astype(BF16)


def _attn_in(h, g, w, wf, fb, gains, *, tm):
    b, s, d = h.shape
    n = w.shape[1]
    return pl.pallas_call(
        _attn_in_body,
        out_shape=(jax.ShapeDtypeStruct((b, s, n), BF16),
                   jax.ShapeDtypeStruct((b, s, LANES), F32)),
        grid=(b, s // tm),
        in_specs=[
            pl.BlockSpec((None, tm, d), lambda bi, i: (bi, i, 0)),
            _resident((1, d)),
            _resident((d, n)),
            _resident((d, LANES)),
            _resident((1, LANES)),
            _resident((4, LANES)),
        ],
        out_specs=(pl.BlockSpec((None, tm, n), lambda bi, i: (bi, i, 0)),
                   pl.BlockSpec((None, tm, LANES), lambda bi, i: (bi, i, 0))),
        scratch_shapes=[pltpu.VMEM((8, LANES), F32)],
        compiler_params=_cparams(2, sequential=True),
        name="attn_in",
    )(h, g, w, wf, fb, gains)


def _swa_block(sinks_ref, q, kv, mask):
    blk = SWA_BLOCK
    nkeys = kv.shape[0]
    lo = _lane_lo((blk, LANES))
    ones = jnp.ones((nkeys, LANES), BF16)
    mask2 = jnp.concatenate([mask, mask], axis=0)
    first_head_rows = lax.broadcasted_iota(jnp.int32, (2 * blk, 1), 0) < blk
    outs = {}
    for kv_head in range(SWA_KV_HEADS):
        tiles = (2 * kv_head, 2 * kv_head + 1)
        for j in range(2):
            var = 0 if kv_head == j else 1
            kk = kv[:, var * LANES:(var + 1) * LANES]
            v_aug = jnp.concatenate([kv[:, (2 + var) * LANES:(3 + var) * LANES], ones], axis=1)
            sel = lo if j == 0 else ~lo
            qm = jnp.concatenate(
                [jnp.where(sel, q[:, t * LANES:(t + 1) * LANES], jnp.zeros((blk, LANES), BF16))
                 for t in tiles], axis=0)
            sink = jnp.where(first_head_rows, sinks_ref[2 * tiles[0] + j] * LOG2E,
                             sinks_ref[2 * tiles[1] + j] * LOG2E)
            sc = jnp.where(mask2, _dot_nt(qm, kk), -jnp.inf)
            m = jnp.maximum(jnp.max(sc, axis=-1, keepdims=True), sink)
            pv = _dot(jnp.exp2(sc - m).astype(BF16), v_aug)
            o = pv[:, :LANES] / (pv[:, LANES:] + jnp.exp2(sink - m))
            outs[tiles[0], j] = o[:blk, :]
            outs[tiles[1], j] = o[blk:, :]
    return jnp.concatenate([jnp.where(lo, outs[t, 0], outs[t, 1])
                            for t in range(SWA_Q_HEADS // 2)], axis=1).astype(BF16)


def _swa_body(sinks_ref, q_ref, kv_ref, o_ref):
    blk = SWA_BLOCK
    r = lax.broadcasted_iota(jnp.int32, (blk, 2 * blk), 0)
    c = lax.broadcasted_iota(jnp.int32, (blk, 2 * blk), 1)
    o_ref[0:blk, :] = _swa_block(sinks_ref, q_ref[0:blk, :], kv_ref[0:blk, :], _lower_tri(blk))
    window = (c > r) & (c <= r + blk)

    def step(n, carry):
        q0 = pl.multiple_of(n * blk, blk)
        k0 = pl.multiple_of((n - 1) * blk, blk)
        o_ref[pl.ds(q0, blk), :] = _swa_block(sinks_ref, q_ref[pl.ds(q0, blk), :],
                                              kv_ref[pl.ds(k0, 2 * blk), :], window)
        return carry

    lax.fori_loop(1, q_ref.shape[0] // blk, step, 0)


def _swa(sinks, qkv):
    b, s, _ = qkv.shape
    qw = SWA_Q_HEADS * HEAD_DIM
    return pl.pallas_call(
        _swa_body,
        out_shape=jax.ShapeDtypeStruct((b, s, qw), BF16),
        grid_spec=pltpu.PrefetchScalarGridSpec(
            num_scalar_prefetch=1,
            grid=(b,),
            in_specs=[
                pl.BlockSpec((None, s, qw), lambda bi, sk: (bi, 0, 0)),
                pl.BlockSpec((None, s, qw), lambda bi, sk: (bi, 0, 1)),
            ],
            out_specs=pl.BlockSpec((None, s, qw), lambda bi, sk: (bi, 0, 0)),
        ),
        compiler_params=_cparams(1),
        name="swa",
    )(sinks, qkv, qkv)


FOX_Q_TILE0 = 8
FOX_K_TILE0 = 12
FOX_V_TILE0 = 16


def _fox_body(q_ref, k_ref, v_ref, fq_ref, fk_ref, o_ref, m_ref, acc_ref, sa_ref, sb_ref, *, tq):
    hp = pl.program_id(1)
    i = pl.program_id(2)
    q = q_ref[...]
    fq_all = fq_ref[...]
    lo = _lane_lo((tq, LANES))
    lane = lax.broadcasted_iota(jnp.int32, (tq, LANES), 1)
    zero = jnp.zeros_like(q)
    q2 = jnp.concatenate([jnp.where(lo, q, zero), jnp.where(lo, zero, q)], axis=0)
    fq0 = jnp.sum(jnp.where(lane == 2 * hp, fq_all, 0.0), axis=-1, keepdims=True)
    fq1 = jnp.sum(jnp.where(lane == 2 * hp + 1, fq_all, 0.0), axis=-1, keepdims=True)
    ones = jnp.ones((tq, LANES), BF16)

    m_ref[...] = jnp.full(m_ref.shape, NEG_BIG, F32)
    acc_ref[...] = jnp.zeros(acc_ref.shape, F32)

    def scores(jb, masked):
        start = pl.multiple_of(jb * tq, tq)
        kb = k_ref[pl.ds(start, tq), :]
        fk = fk_ref[jb]
        sc = _dot_nt(q2, kb) + jnp.concatenate([fq0 - fk[0:1, :], fq1 - fk[1:2, :]], axis=0)
        if masked:
            tri = _lower_tri(tq)
            sc = jnp.where(jnp.concatenate([tri, tri], axis=0), sc, NEG_BIG)
        return sc

    def accumulate(sc, jb):
        start = pl.multiple_of(jb * tq, tq)
        v_aug = jnp.concatenate([v_ref[pl.ds(start, tq), :], ones], axis=1)
        m_old = m_ref[...]
        m_new = jnp.maximum(m_old, jnp.max(sc, axis=-1, keepdims=True))
        alpha = jnp.exp2(m_old - m_new)
        p = jnp.concatenate([jnp.exp2(sc[:, t * LANES:(t + 1) * LANES] - m_new)
                             for t in range(tq // LANES)], axis=1).astype(BF16)
        pv = _dot(p, v_aug)
        acc_ref[:, :LANES] = alpha * acc_ref[:, :LANES] + pv[:, :LANES]
        acc_ref[:, LANES:] = alpha * acc_ref[:, LANES:] + pv[:, LANES:]
        m_ref[...] = m_new

    sa_ref[...] = scores(i, True)

    def visited(n):
        return jnp.where(n == 0, i, n - 1)

    def pair(u, carry):
        sb_ref[...] = scores(2 * u, False)
        accumulate(sa_ref[...], visited(2 * u))
        sa_ref[...] = scores(2 * u + 1, False)
        accumulate(sb_ref[...], 2 * u)
        return carry

    lax.fori_loop(0, i // 2, pair, 0)

    @pl.when(i % 2 == 1)
    def _():
        sb_ref[...] = scores(i - 1, False)
        accumulate(sa_ref[...], visited(i - 1))
        accumulate(sb_ref[...], i - 1)

    @pl.when(i % 2 == 0)
    def _():
        accumulate(sa_ref[...], visited(i))

    acc = acc_ref[...]
    o = acc[:, :LANES] / acc[:, LANES:]
    o_ref[...] = jnp.where(lo, o[:tq, :], o[tq:, :]).astype(BF16)


def _fox(qkv, cum_col, cum_row, cast_jobs, *, tq):
    b, s, _ = qkv.shape
    nq = s // tq
    npairs = FOX_HEADS // 2
    step_of = lambda bi, hp, i: (bi * npairs + hp) * nq + i
    cast_in, cast_out, cast_shapes, cast_blocks = _cast_jobs(cast_jobs, b * npairs * nq, step_of)
    in_specs = [
        pl.BlockSpec((None, tq, LANES), lambda bi, hp, i: (bi, i, FOX_Q_TILE0 + hp)),
        pl.BlockSpec((None, s, LANES), lambda bi, hp, i: (bi, 0, FOX_K_TILE0 + hp)),
        pl.BlockSpec((None, s, LANES), lambda bi, hp, i: (bi, 0, FOX_V_TILE0 + hp)),
        pl.BlockSpec((None, tq, LANES), lambda bi, hp, i: (bi, i, 0)),
        pl.BlockSpec((None, None, nq, 2, tq), lambda bi, hp, i: (bi, hp, 0, 0, 0)),
    ]
    out, *casts = pl.pallas_call(
        _with_casts(functools.partial(_fox_body, tq=tq), len(in_specs), 1, cast_blocks, step_of, 3),
        out_shape=[jax.ShapeDtypeStruct((b, s, FOX_HEADS * HEAD_DIM), BF16)] + cast_shapes,
        grid=(b, npairs, nq),
        in_specs=in_specs + cast_in,
        out_specs=[pl.BlockSpec((None, tq, LANES), lambda bi, hp, i: (bi, i, hp))] + cast_out,
        scratch_shapes=[pltpu.VMEM((2 * tq, LANES), F32), pltpu.VMEM((2 * tq, 2 * LANES), F32),
                        pltpu.VMEM((2 * tq, tq), F32), pltpu.VMEM((2 * tq, tq), F32)],
        compiler_params=_cparams(3, sequential=True),
        name="fox",
    )(qkv, qkv, qkv, cum_col, cum_row, *(arr for arr, _ in cast_jobs))
    return out, casts


SSM_CONV_BLOCK = 512
SSM_PROJ_BLOCK = 256


def _ssm_project_pieces(h_ref, g_ref, wz_ref, wx_ref, wdt_ref, z_buf, x_buf, dt_buf):
    u = _rmsnorm(h_ref[...], g_ref[...]).astype(BF16)

    def piece(buf, w_ref, c0, c1):
        def run():
            buf[:, c0:c1] = _dot(u, w_ref[:, c0:c1])
        return run

    pieces = [piece(dt_buf, wdt_ref, 0, LANES)]
    for buf, w_ref in ((x_buf, wx_ref), (z_buf, wz_ref)):
        n = w_ref.shape[1]
        pieces += [piece(buf, w_ref, c0, min(c0 + SSM_PROJ_BLOCK, n))
                   for c0 in range(0, n, SSM_PROJ_BLOCK)]
    return pieces


def _ssm_mix(hprev_ref, z_buf, x_buf, dt_buf, cw_ref, cb_ref, dtb_ref, a_ref, dsk_ref, ng_ref,
             wout_ref, o_ref, state_ref, tail_ref, side_work, *, tm):
    d_inner = SSM_HEADS * HEAD_DIM
    bc_w = SSM_GROUPS * SSM_STATE
    L = SSM_CHUNK
    stages = (tm // L) * SSM_GROUPS
    per_stage = -(-len(side_work) // stages)

    def do_side_work(n):
        for _ in range(n):
            if side_work:
                side_work.pop(0)()

    conv_w = x_buf.shape[1]
    row8 = lax.broadcasted_iota(jnp.int32, (8, SSM_CONV_BLOCK), 0)
    xbc_blocks = []
    for c0 in range(0, conv_w, SSM_CONV_BLOCK):
        cols = slice(c0, c0 + SSM_CONV_BLOCK)
        cur = x_buf[:, cols]
        prev = tail_ref[:, cols]
        conv = cur * cw_ref[SSM_CONV - 1:SSM_CONV, cols] + cb_ref[:, cols]
        for sh in range(1, SSM_CONV):
            cur_s = pltpu.roll(cur, sh, axis=0)
            prev_s = pltpu.roll(prev, sh, axis=0)
            top = jnp.where(row8 < sh, prev_s, cur_s[0:8, :])
            shifted = jnp.concatenate([top, cur_s[8:, :]], axis=0)
            conv = conv + shifted * cw_ref[SSM_CONV - 1 - sh:SSM_CONV - sh, cols]
        tail_ref[:, cols] = cur[tm - 8:tm, :]
        xbc_blocks.append(conv * _sigmoid(conv))
    xbc =jnp.concatenate(xbc_blocks, axis=1)

    a2 = -jnp.exp(a_ref[...]) * LOG2E
    dt_all = _softplus(dt_buf[...] + dtb_ref[...])
    tri = _lower_tri(L)
    tri_bf = tri.astype(BF16)
    lo = _lane_lo((L, LANES))
    lo_row = _lane_lo((1, LANES))
    heads_per_group = SSM_HEADS // SSM_GROUPS

    for c in range(tm // L):
        rows = slice(c * L, (c + 1) * L)
        dt = dt_all[rows, :]
        a_cs = _cumsum_rows(dt * a2, tri_bf, one_matmul=False)
        a_cs_t = a_cs.T
        r_t = a_cs_t - jnp.log2(dt.T)
        y_tiles = []
        for g in range(SSM_GROUPS):
            b_g = xbc[rows, d_inner + g * SSM_STATE:d_inner + (g + 1) * SSM_STATE]
            c_g = xbc[rows, d_inner + bc_w + g * SSM_STATE:d_inner + bc_w + (g + 1) * SSM_STATE]
            cb = _dot_nt(c_g.astype(BF16), b_g.astype(BF16))
            b_t = b_g.T
            for pr in range(heads_per_group // 2):
                hp = g * (heads_per_group // 2) + pr
                x_pair = xbc[rows, hp * LANES:(hp + 1) * LANES]
                xb = x_pair.astype(BF16)
                st = state_ref[hp]
                lhs_y, lhs_s, cds = [], [], []
                for j in range(2):
                    h = 2 * hp + j
                    a_col = jnp.broadcast_to(a_cs[:, h:h + 1], (L, L))
                    r_row = r_t[h:h + 1, :]
                    a_last = a_cs_t[h:h + 1, L - 1:L]
                    m_diag = cb * jnp.exp2(jnp.where(tri, a_col - r_row, -jnp.inf))
                    c_off = c_g * jnp.exp2(a_col)
                    lhs_y.append(jnp.concatenate([m_diag, c_off], axis=1).astype(BF16))
                    lhs_s.append((b_t * jnp.exp2(a_last - r_row)).astype(BF16))
                    cds.append(jnp.exp2(a_last))
                rhs = jnp.concatenate([xb, st.astype(BF16)], axis=0)
                y2 = _dot(jnp.concatenate(lhs_y, axis=0), rhs)
                y_tiles.append(jnp.where(lo, y2[0:L, :], y2[L:2 * L, :])
                               + dsk_ref[:, hp * LANES:(hp + 1) * LANES] * x_pair)
                s2 = _dot(jnp.concatenate(lhs_s, axis=0), xb)
                cd = jnp.where(lo_row, cds[0], cds[1])
                state_ref[hp] = st * cd + jnp.where(lo, s2[0:SSM_STATE, :], s2[SSM_STATE:, :])
            do_side_work(per_stage)
        y = jnp.concatenate(y_tiles, axis=1)
        zc = z_buf[rows, :]
        y = y * (zc * _sigmoid(zc))
        gw = d_inner // SSM_GROUPS
        normed = []
        for g in range(SSM_GROUPS):
            yg = y[:, g * gw:(g + 1) * gw]
            ms = jnp.mean(yg * yg, axis=-1, keepdims=True)
            normed.append(yg * lax.rsqrt(ms + RMS_EPS))
        y_out = (jnp.concatenate(normed, axis=1) * ng_ref[...]).astype(BF16)
        o_ref[rows, :] = hprev_ref[rows, :] + _dot(y_out, wout_ref[...])

    do_side_work(len(side_work))


def _ssm_body(h_ref, hprev_ref, g_ref, wz_ref, wx_ref, wdt_ref, cw_ref, cb_ref, dtb_ref, a_ref,
              dsk_ref, ng_ref, wout_ref, o_ref, z0, x0, d0, z1, x1, d1, state_ref, tail_ref,
              *, tm, tiles_per_seq):
    t = pl.program_id(0)

    @pl.when(t == 0)
    def _():
        z1[...] = jnp.zeros_like(z1)
        x1[...] = jnp.zeros_like(x1)
        d1[...] = jnp.zeros_like(d1)

    @pl.when((t == 0) | ((t - 1) % tiles_per_seq == 0))
    def _():
        state_ref[...] = jnp.zeros_like(state_ref)
        tail_ref[...] = jnp.zeros_like(tail_ref)

    def run(pbuf, mbuf):
        pieces = _ssm_project_pieces(h_ref, g_ref, wz_ref, wx_ref, wdt_ref, *pbuf)
        _ssm_mix(hprev_ref, *mbuf, cw_ref, cb_ref, dtb_ref, a_ref, dsk_ref, ng_ref, wout_ref,
                 o_ref, state_ref, tail_ref, pieces, tm=tm)

    @pl.when(t % 2 == 0)
    def _():
        run((z0, x0, d0), (z1, x1, d1))

    @pl.when(t % 2 == 1)
    def _():
        run((z1, x1, d1), (z0, x0, d0))


def _ssm_fused(h, g, wz, wx, wdt, cw, cb, dtb, a_log, dsk, ng, wout, cast_jobs, *, tm, seq):
    t, d = h.shape
    d_inner, conv_w = wz.shape[1], wx.shape[1]
    n_tiles = t // tm
    step_of = lambda i: i
    cast_in, cast_out, cast_shapes, cast_blocks = _cast_jobs(cast_jobs, n_tiles + 1, step_of)
    in_specs = [
        pl.BlockSpec((tm, d), lambda i: (jnp.minimum(i, n_tiles - 1), 0)),
        pl.BlockSpec((tm, d), lambda i: (jnp.maximum(i - 1, 0), 0)),
        _resident((1, d)),
        _resident((d, d_inner)),
        _resident((d, conv_w)),
        _resident((d, LANES)),
        _resident((SSM_CONV, conv_w)),
        _resident((1, conv_w)),
        _resident((1, LANES)),
        _resident((1, LANES)),
        _resident((1, d_inner)),
        _resident((1, d_inner)),
        _resident((d_inner, d)),
    ]
    out, *casts = pl.pallas_call(
        _with_casts(functools.partial(_ssm_body, tm=tm, tiles_per_seq=seq // tm),
                    len(in_specs), 1, cast_blocks, step_of, 1),
        out_shape=[jax.ShapeDtypeStruct((t, d), F32)] + cast_shapes,
        grid=(n_tiles + 1,),
        in_specs=in_specs + cast_in,
        out_specs=[pl.BlockSpec((tm, d), lambda i: (jnp.maximum(i - 1, 0), 0))] + cast_out,
        scratch_shapes=[
            pltpu.VMEM((tm, d_inner), F32), pltpu.VMEM((tm, conv_w), F32), pltpu.VMEM((tm, LANES), F32),
            pltpu.VMEM((tm, d_inner), F32), pltpu.VMEM((tm, conv_w), F32), pltpu.VMEM((tm, LANES), F32),
            pltpu.VMEM((SSM_HEADS // 2, SSM_STATE, LANES), F32),
            pltpu.VMEM((8, conv_w), F32),
        ],
        compiler_params=_cparams(1, sequential=True),
        name="ssm",
    )(h, h, g, wz, wx, wdt, cw, cb, dtb, a_log, dsk, ng, wout, *(arr for arr, _ in cast_jobs))
    return out, casts


def _pad_lanes(v):
    v = v.reshape(1, -1).astype(F32)
    return jnp.pad(v, ((0, 0), (0, LANES - v.shape[1])))


def _swap_heads(w):
    return jnp.concatenate([w[:, HEAD_DIM:], w[:, :HEAD_DIM]], axis=1)


def _attention_layer(h3, g, w_in, forget_bias, swa_qn, swa_kn, sinks, fox_qn, fox_kn, w_out,
                     cast_jobs, *, tm_in, tq):
    b, s, d = h3.shape
    qa_w = SWA_Q_HEADS * HEAD_DIM
    kv_w = SWA_KV_HEADS * HEAD_DIM
    fox_w = FOX_HEADS * HEAD_DIM
    o = 0
    w_qa = w_in[:, o:o + qa_w]; o += qa_w
    w_ka = w_in[:, o:o + kv_w]; o += kv_w
    w_va = w_in[:, o:o + kv_w]; o += kv_w
    w_fox = w_in[:, o:o + 3 * fox_w]; o += 3 * fox_w
    w_f = w_in[:, o:]
    w_main = jnp.concatenate(
        [w_qa, w_ka, _swap_heads(w_ka), w_va, _swap_heads(w_va), w_fox], axis=1).astype(BF16)
    w_fp = jnp.pad(w_f, ((0, 0), (0, LANES - w_f.shape[1]))).astype(BF16)
    gains = jnp.stack([jnp.tile(v.astype(F32), 2) for v in (swa_qn, swa_kn, fox_qn, fox_kn)])
    qkv, cum = _attn_in(h3, g.reshape(1, d), w_main, w_fp, _pad_lanes(forget_bias), gains, tm=tm_in)
    o_a = _swa(sinks.astype(F32), qkv)
    nq = s // tq
    cum_row = cum[:, :, :FOX_HEADS].reshape(b, nq, tq, FOX_HEADS // 2, 2).transpose(0, 3, 1, 4, 2)
    o_b, casts = _fox(qkv, cum, cum_row, cast_jobs, tq=tq)
    w_out_b = w_out.astype(BF16)
    mixer_out = (o_a.reshape(b * s, qa_w), o_b.reshape(b * s, fox_w), w_out_b[:qa_w], w_out_b[qa_w:])
    return mixer_out, casts


def _ssm_layer(h3, g, w_in, conv_w, conv_b, dt_bias, a_log, d_skip, norm_g, w_out,
               cast_jobs, *, tm):
    b, s, d = h3.shape
    d_inner = SSM_HEADS * HEAD_DIM
    cw = d_inner + 2 * SSM_GROUPS * SSM_STATE
    w_z = w_in[:, :d_inner].astype(BF16)
    w_x = w_in[:, d_inner:d_inner + cw].astype(BF16)
    w_dt = w_in[:, d_inner + cw:]
    w_dt = jnp.pad(w_dt, ((0, 0), (0, LANES - w_dt.shape[1]))).astype(BF16)
    dsk = jnp.repeat(d_skip.astype(F32), HEAD_DIM).reshape(1, d_inner)
    out, casts = _ssm_fused(h3.reshape(b * s, d), g.reshape(1, d), w_z, w_x, w_dt,
                            conv_w.astype(F32), conv_b.reshape(1, cw).astype(F32),
                            _pad_lanes(dt_bias), _pad_lanes(a_log), dsk,
                            norm_g.reshape(1, d_inner).astype(F32), w_out.astype(BF16),
                            cast_jobs, tm=tm, seq=s)
    return out.reshape(b, s, d), casts


def kernel(x, ffn1_norm, ffn1_w_gate, ffn1_w_up, ffn1_w_down, mix_norm, ffn2_norm, ffn2_w_gate, ffn2_w_up, ffn2_w_down, attn_w_in, attn_forget_bias, swa_q_norm, swa_k_norm, swa_sinks, fox_q_norm, fox_k_norm, attn_w_out, ssm_w_in, ssm_conv_w, ssm_conv_b, ssm_dt_bias, ssm_a_log, ssm_d_skip, ssm_norm, ssm_w_out):
    b, s, d = x.shape
    depth = ffn1_norm.shape[0]
    t = b * s
    tm = min(512, t)
    tq = min(512, s)

    ffn1_w = (ffn1_w_gate, ffn1_w_up, ffn1_w_down)
    ffn2_w = (ffn2_w_gate, ffn2_w_up, ffn2_w_down)

    def ffn(h3, g, weights, mixer_out=None):
        return _ffn(h3.reshape(t, d), g.reshape(1, d), *weights, tm=tm,
                    mixer_out=mixer_out).reshape(b, s, d)

    w1 = [w[0].astype(BF16) for w in ffn1_w]
    h = x
    for layer in range(depth):
        h = ffn(h, ffn1_norm[layer], w1)
        cast_jobs = [(w, layer) for w in ffn2_w]
        if layer + 1 < depth:
            cast_jobs += [(w, layer + 1) for w in ffn1_w]
        i = layer // 2
        mixer_out = None
        if layer % 2 == 0:
            mixer_out, casts = _attention_layer(
                h, mix_norm[layer], attn_w_in[i], attn_forget_bias[i], swa_q_norm[i],
                swa_k_norm[i], swa_sinks[i], fox_q_norm[i], fox_k_norm[i], attn_w_out[i],
                cast_jobs, tm_in=min(512, s), tq=tq)
        else:
            h, casts = _ssm_layer(h, mix_norm[layer], ssm_w_in[i], ssm_conv_w[i], ssm_conv_b[i],
                                  ssm_dt_bias[i], ssm_a_log[i], ssm_d_skip[i], ssm_norm[i],
                                  ssm_w_out[i], cast_jobs, tm=min(256, s))
        w2, w1 = casts[:3], casts[3:]
        h = ffn(h, ffn2_norm[layer], w2, mixer_out)
    return h
```
